```python
import math
import jax, jax.numpy as jnp
from jax import lax
import numpy as np

D_MODEL = 2048
BATCH = 8
SEQ = 4096
DEPTH = 4

N_A_LAYERS = DEPTH // 2
N_B_LAYERS = DEPTH - N_A_LAYERS
GDN_HEADS = 16
GDN_DK = 128
GDN_DV = 128
GDN_QK_WIDTH = GDN_HEADS * GDN_DK
GDN_V_WIDTH = GDN_HEADS * GDN_DV
GDN_CONV_CH = 2 * GDN_QK_WIDTH + GDN_V_WIDTH
GDN_PROJ = GDN_CONV_CH + GDN_V_WIDTH + 2 * GDN_HEADS
CONV_WIDTH = 4
CHUNK = 64
GDN_EPS = 1e-6
DIFF_HEADS = 16
DIFF_DK = D_MODEL // DIFF_HEADS // 2
DIFF_DV = 2 * DIFF_DK
DIFF_Q_WIDTH = 2 * DIFF_HEADS * DIFF_DK
DIFF_V_WIDTH = DIFF_HEADS * DIFF_DV
Q_BLOCK = 128
SUBLN_EPS = 1e-5
D_FF = 4 * D_MODEL
ALPHA = (2 * DEPTH) ** 0.25
BETA = (8 * DEPTH) ** -0.25
LN_EPS = 1e-5

kernel_name = "yoco_gdn_diffattn_hybrid"


def layer_norm(x, g, b):
    xf = x.astype(jnp.float32)
    mu = jnp.mean(xf, axis=-1, keepdims=True)
    var = jnp.mean(jnp.square(xf - mu), axis=-1, keepdims=True)
    return ((xf - mu) * lax.rsqrt(var + LN_EPS) * g + b).astype(x.dtype)


def rms_norm(x, w, eps):
    xf = x.astype(jnp.float32)
    return (xf * lax.rsqrt(jnp.mean(jnp.square(xf), axis=-1, keepdims=True) + eps) * w).astype(x.dtype)


def l2_normalize(x):
    xf = x.astype(jnp.float32)
    return xf * lax.rsqrt(jnp.sum(jnp.square(xf), axis=-1, keepdims=True) + GDN_EPS)


def causal_conv_silu(u, w):
    S = u.shape[1]
    K = w.shape[0]
    up = jnp.pad(u, ((0, 0), (K - 1, 0), (0, 0)))
    y = up[:, 0:S] * w[0]
    for j in range(1, K):
        y = y + up[:, j:j + S] * w[j]
    return jax.nn.silu(y)


def gated_delta_rule_chunked(q, k, v, g, beta):
    B, H, S, dk = q.shape
    dv = v.shape[-1]
    N = S // CHUNK
    q = q.reshape(B, H, N, CHUNK, dk)
    k = k.reshape(B, H, N, CHUNK, dk)
    v = v.reshape(B, H, N, CHUNK, dv)
    beta = beta.reshape(B, H, N, CHUNK)
    g = jnp.cumsum(g.reshape(B, H, N, CHUNK), axis=-1)
    causal = jnp.tril(jnp.ones((CHUNK, CHUNK), dtype=bool))
    strict = jnp.tril(jnp.ones((CHUNK, CHUNK), dtype=bool), k=-1)
    decay = jnp.exp(jnp.where(causal, g[..., :, None] - g[..., None, :], -jnp.inf))
    k_beta = k * beta[..., None]
    lower = jnp.where(strict, jnp.einsum('bhnid,bhnjd->bhnij', k_beta, k) * decay, 0.0)
    tri = lower + jnp.eye(CHUNK, dtype=lower.dtype)
    rhs = jnp.concatenate([v * beta[..., None], k_beta * jnp.exp(g)[..., None]], axis=-1)
    sol = lax.linalg.triangular_solve(tri, rhs, left_side=True, lower=True, unit_diagonal=True)
    u, w = sol[..., :dv], sol[..., dv:]
    attn_intra = jnp.einsum('bhnid,bhnjd->bhnij', q, k) * decay

    def step(state, inp):
        q_n, k_n, u_n, w_n, g_n, a_n = inp
        v_new = u_n - jnp.einsum('bhck,bhkv->bhcv', w_n, state)
        o = (jnp.einsum('bhck,bhkv->bhcv', q_n * jnp.exp(g_n)[..., None], state)
             + jnp.einsum('bhij,bhjv->bhiv', a_n, v_new))
        g_last = g_n[..., -1]
        state = (state * jnp.exp(g_last)[..., None, None]
                 + jnp.einsum('bhck,bhcv->bhkv', k_n * jnp.exp(g_last[..., None] - g_n)[..., None], v_new))
        return state, o

    to_front = lambda t: jnp.moveaxis(t, 2, 0)
    xs = (to_front(q), to_front(k), to_front(u), to_front(w), to_front(g), to_front(attn_intra))
    state0 = jnp.zeros((B, H, dk, dv), jnp.float32)
    _, o = lax.scan(step, state0, xs)
    return jnp.moveaxis(o, 0, 2).reshape(B, H, S, dv)


def gated_deltanet(x, w_in, conv_w, a_log, dt_bias, norm_w, w_out):
    B, S, _ = x.shape
    p = x @ w_in
    qkv = causal_conv_silu(p[..., :GDN_CONV_CH], conv_w)
    z = p[..., GDN_CONV_CH:GDN_CONV_CH + GDN_V_WIDTH]
    b = p[..., GDN_CONV_CH + GDN_V_WIDTH:GDN_CONV_CH + GDN_V_WIDTH + GDN_HEADS]
    a = p[..., GDN_CONV_CH + GDN_V_WIDTH + GDN_HEADS:]
    q = l2_normalize(qkv[..., :GDN_QK_WIDTH].reshape(B, S, GDN_HEADS, GDN_DK)) * (GDN_DK ** -0.5)
    k = l2_normalize(qkv[..., GDN_QK_WIDTH:2 * GDN_QK_WIDTH].reshape(B, S, GDN_HEADS, GDN_DK))
    v = qkv[..., 2 * GDN_QK_WIDTH:].reshape(B, S, GDN_HEADS, GDN_DV).astype(jnp.float32)
    beta = jax.nn.sigmoid(b.astype(jnp.float32))
    g = -jnp.exp(a_log.astype(jnp.float32)) * jax.nn.softplus(a.astype(jnp.float32) + dt_bias.astype(jnp.float32))
    tr = lambda t: jnp.swapaxes(t, 1, 2)
    o = gated_delta_rule_chunked(tr(q), tr(k), tr(v), tr(g), tr(beta))
    o = jnp.swapaxes(o, 1, 2).astype(x.dtype)
    o = rms_norm(o, norm_w, GDN_EPS) * jax.nn.silu(z.reshape(B, S, GDN_HEADS, GDN_DV))
    return o.reshape(B, S, GDN_V_WIDTH) @ w_out


def shared_kv(x, w_kv):
    B, S, _ = x.shape
    kv = x @ w_kv
    k = kv[..., :DIFF_Q_WIDTH].reshape(B, S, DIFF_HEADS, 2, DIFF_DK).transpose(0, 2, 3, 1, 4)
    v = kv[..., DIFF_Q_WIDTH:].reshape(B, S, DIFF_HEADS, DIFF_DV).transpose(0, 2, 1, 3)
    return k, v


def diff_attention(x, w_q, lam_params, subln_w, w_o, k_sh, v_sh, lambda_init):
    B, S, _ = x.shape
    nb = S // Q_BLOCK
    q = (x @ w_q).reshape(B, S, DIFF_HEADS, 2, DIFF_DK).transpose(0, 2, 3, 1, 4)
    qb = q.reshape(B, DIFF_HEADS, 2, nb, Q_BLOCK, DIFF_DK).transpose(3, 0, 1, 2, 4, 5)
    lp = lam_params.astype(jnp.float32)
    lam = jnp.exp(jnp.sum(lp[0] * lp[1])) - jnp.exp(jnp.sum(lp[2] * lp[3])) + lambda_init
    k_pos = jnp.arange(S)
    scale = DIFF_DK ** -0.5

    def block(args):
        q_i, i = args
        s = jnp.einsum('bhcqd,bhckd->bhcqk', q_i, k_sh).astype(jnp.float32) * scale
        q_pos = i * Q_BLOCK + jnp.arange(Q_BLOCK)
        s = jnp.where(k_pos[None, :] <= q_pos[:, None], s, -jnp.inf)
        p = jax.nn.softmax(s, axis=-1)
        attn = p[:, :, 0] - lam * p[:, :, 1]
        return jnp.einsum('bhqk,bhkv->bhqv', attn.astype(v_sh.dtype), v_sh)

    o = lax.map(block, (qb, jnp.arange(nb)))
    o = o.transpose(1, 2, 0, 3, 4).reshape(B, DIFF_HEADS, S, DIFF_DV)
    o = rms_norm(o, subln_w, SUBLN_EPS) * (1.0 - lambda_init)
    return o.transpose(0, 2, 1, 3).reshape(B, S, DIFF_V_WIDTH) @ w_o


def sq_relu_mlp(x, w_up, w_down):
    return jnp.square(jax.nn.relu(x @ w_up)) @ w_down


def setup_inputs(seed: int = 0) -> dict:
    key = jax.random.key(seed)
    ks = jax.random.split(key, 16)
    nrm = lambda k, shape, scale: jax.random.normal(k, shape, jnp.float32) * scale
    x = nrm(ks[0], (BATCH, SEQ, D_MODEL), 1.0)
    gdn_w_in = nrm(ks[1], (N_A_LAYERS, D_MODEL, GDN_PROJ), D_MODEL ** -0.5)
    gdn_conv_w = nrm(ks[2], (N_A_LAYERS, CONV_WIDTH, GDN_CONV_CH), CONV_WIDTH ** -0.5)
    gdn_a_log = jnp.log(jax.random.uniform(ks[3], (N_A_LAYERS, GDN_HEADS), jnp.float32, 1.0, 16.0))
    dt = jnp.exp(jax.random.uniform(ks[4], (N_A_LAYERS, GDN_HEADS), jnp.float32, math.log(1e-3), math.log(1e-1)))
    gdn_dt_bias = dt + jnp.log(-jnp.expm1(-dt))
    gdn_norm_w = 1.0 + nrm(ks[5], (N_A_LAYERS, GDN_DV), 0.02)
    gdn_w_out = nrm(ks[6], (N_A_LAYERS, GDN_V_WIDTH, D_MODEL), GDN_V_WIDTH ** -0.5 * BETA)
    diff_w_q = nrm(ks[7], (N_B_LAYERS, D_MODEL, DIFF_Q_WIDTH), D_MODEL ** -0.5)
    diff_lambda = nrm(ks[8], (N_B_LAYERS, 4, DIFF_DK), 0.1)
    diff_subln_w = 1.0 + nrm(ks[9], (N_B_LAYERS, DIFF_DV), 0.02)
    diff_w_o = nrm(ks[10], (N_B_LAYERS, DIFF_V_WIDTH, D_MODEL), DIFF_V_WIDTH ** -0.5 * BETA)
    shared_w_kv = nrm(ks[11], (D_MODEL, DIFF_Q_WIDTH + DIFF_V_WIDTH), D_MODEL ** -0.5)
    mlp_w_up = nrm(ks[12], (DEPTH, D_MODEL, D_FF), D_MODEL ** -0.5)
    mlp_w_down = nrm(ks[13], (DEPTH, D_FF, D_MODEL), D_FF ** -0.5 * BETA)
    ln_g = 1.0 + nrm(ks[14], (DEPTH, 2, D_MODEL), 0.02)
    ln_b = nrm(ks[15], (DEPTH, 2, D_MODEL), 0.02)
    return {"x": x, "gdn_w_in": gdn_w_in, "gdn_conv_w": gdn_conv_w, "gdn_a_log": gdn_a_log,
            "gdn_dt_bias": gdn_dt_bias, "gdn_norm_w": gdn_norm_w, "gdn_w_out": gdn_w_out,
            "diff_w_q": diff_w_q, "diff_lambda": diff_lambda, "diff_subln_w": diff_subln_w,
            "diff_w_o": diff_w_o, "shared_w_kv": shared_w_kv, "mlp_w_up": mlp_w_up,
            "mlp_w_down": mlp_w_down, "ln_g": ln_g, "ln_b": ln_b}


def reference(x, gdn_w_in, gdn_conv_w, gdn_a_log, gdn_dt_bias, gdn_norm_w, gdn_w_out,
              diff_w_q, diff_lambda, diff_subln_w, diff_w_o, shared_w_kv, mlp_w_up,
              mlp_w_down, ln_g, ln_b):
    k_sh = None
    v_sh = None
    for l in range(DEPTH):
        if l < N_A_LAYERS:
            h = gated_deltanet(x, gdn_w_in[l], gdn_conv_w[l], gdn_a_log[l], gdn_dt_bias[l],
                               gdn_norm_w[l], gdn_w_out[l])
        else:
            j = l - N_A_LAYERS
            lambda_init = 0.8 - 0.6 * math.exp(-0.3 * l)
            h = diff_attention(x, diff_w_q[j], diff_lambda[j], diff_subln_w[j], diff_w_o[j],
                               k_sh, v_sh, lambda_init)
        x = layer_norm(ALPHA * x + h, ln_g[l, 0], ln_b[l, 0])
        x = layer_norm(ALPHA * x + sq_relu_mlp(x, mlp_w_up[l], mlp_w_down[l]), ln_g[l, 1], ln_b[l, 1])
        if l == N_A_LAYERS - 1:
            k_sh, v_sh = shared_kv(x, shared_w_kv)
    return x
```

```python
import functools
import math

import jax
import jax.numpy as jnp
from jax import lax
from jax.experimental import pallas as pl
from jax.experimental.pallas import tpu as pltpu

F32 = jnp.float32
BF16 = jnp.bfloat16

HEAD_W = 128
CHUNK = 64
CONV_TAPS = 4
GDN_EPS = 1e-6
SUBLN_EPS = 1e-5
LN_EPS = 1e-5
GATE_W = 128

V7X_SUBLANES = 8
V7X_VMEM_LIMIT_BYTES = 56 * 1024 * 1024


def _cparams(sem):
    return pltpu.CompilerParams(dimension_semantics=sem, vmem_limit_bytes=V7X_VMEM_LIMIT_BYTES)


def _bdot(a, b):
    return jnp.dot(a.astype(BF16), b.astype(BF16), preferred_element_type=F32)


def _bdot_nt(a, b):
    return lax.dot_general(a.astype(BF16), b.astype(BF16), (((1,), (1,)), ((), ())),
                           preferred_element_type=F32)


def _bdot_tn(a, b):
    return lax.dot_general(a.astype(BF16), b.astype(BF16), (((0,), (0,)), ((), ())),
                           preferred_element_type=F32)


def _split3(x):
    h = x.astype(BF16)
    r = x - h.astype(F32)
    m = r.astype(BF16)
    l = (r - m.astype(F32)).astype(BF16)
    return h, m, l


def _mm_kernel(x_ref, w_ref, o_ref, *, scale):
    acc = jnp.dot(x_ref[...].astype(BF16), w_ref[...], preferred_element_type=F32)
    if scale != 1.0:
        acc = acc * scale
    o_ref[...] = acc.astype(o_ref.dtype)


def _matmul(x, w, out_dtype, tm, tn, scale=1.0, name="matmul"):
    m, k = x.shape
    n = w.shape[1]
    tm = min(tm, m)
    tn = min(tn, n)
    return pl.pallas_call(
        functools.partial(_mm_kernel, scale=scale),
        grid=(m // tm, n // tn),
        in_specs=[pl.BlockSpec((tm, k), lambda i, j: (i, 0)),
                  pl.BlockSpec((k, tn), lambda i, j: (0, j))],
        out_specs=pl.BlockSpec((tm, tn), lambda i, j: (i, j)),
        out_shape=jax.ShapeDtypeStruct((m, n), out_dtype),
        compiler_params=_cparams(("parallel", "parallel")),
        name=name,
    )(x, w)


def _gate_kernel(x_ref, w_ref, isa_ref, nega_ref, dtb_ref, o_ref):
    x = x_ref[...]
    w = w_ref[...]
    xh = x.astype(BF16)
    xl = (x - xh.astype(F32)).astype(BF16)
    wh = w.astype(BF16)
    wl = (w - wh.astype(F32)).astype(BF16)
    acc = (jnp.dot(xh, wh, preferred_element_type=F32)
           + jnp.dot(xl, wh, preferred_element_type=F32)
           + jnp.dot(xh, wl, preferred_element_type=F32))
    beta = jax.nn.sigmoid(acc)
    s = acc + dtb_ref[...]
    softplus = jnp.maximum(s, 0.0) + jnp.log1p(jnp.exp(-jnp.abs(s)))
    g = nega_ref[...] * softplus
    o_ref[...] = jnp.where(isa_ref[...] > 0.5, g, beta)


def _gates(x2, w_gate, is_a, neg_a, dtb, tm):
    t, d = x2.shape
    tm = min(tm, t)
    vec = pl.BlockSpec((1, GATE_W), lambda i: (0, 0))
    return pl.pallas_call(
        _gate_kernel,
        grid=(t // tm,),
        in_specs=[pl.BlockSpec((tm, d), lambda i: (i, 0)),
                  pl.BlockSpec((d, GATE_W), lambda i: (0, 0)), vec, vec, vec],
        out_specs=pl.BlockSpec((tm, GATE_W), lambda i: (i, 0)),
        out_shape=jax.ShapeDtypeStruct((t, GATE_W), F32),
        compiler_params=_cparams(("parallel",)),
        name="gdn_gates",
    )(x2, w_gate, is_a, neg_a, dtb)


def _unit_lower_inverse_minus_identity(low, same16, same32):
    c = low.shape[0]
    d = jnp.where(same16, low, 0.0)
    x = -d
    p = _bdot(d, d)
    xp = _bdot(jnp.concatenate([x, p], axis=0), p)
    n = x + p + xp[:c]
    p2 = xp[c:]
    np2 = _bdot(jnp.concatenate([n, p2], axis=0), p2)
    n = n + p2 + np2[:c]
    p4 = np2[c:]
    n = n + p4 + _bdot(n, p4)
    off1 = jnp.where(jnp.logical_and(same32, jnp.logical_not(same16)), low, 0.0)
    c1 = off1 + _bdot(n, off1)
    n = n - (c1 + _bdot(c1, n))
    off2 = jnp.where(same32, 0.0, low)
    c2 = off2 + _bdot(n, off2)
    n = n - (c2 + _bdot(c2, n))
    return n


def _gdn_kernel(pq_ref, pk_ref, pv_ref, pz_ref, cwq_ref, cwk_ref, cwv_ref, gcol_ref, grow_ref,
                nw_ref, o_ref, hq_ref, hk_ref, hv_ref, state_ref, *, hb):
    c_idx = pl.program_id(2)
    C = CHUNK
    H0 = V7X_SUBLANES
    w_blk = hb * HEAD_W

    @pl.when(c_idx == 0)
    def _():
        state_ref[...] = jnp.zeros_like(state_ref)
        zero = jnp.zeros((H0, w_blk), F32)
        hq_ref[0:H0, :] = zero
        hk_ref[0:H0, :] = zero
        hv_ref[0:H0, :] = zero

    def conv_silu(p_ref, h_ref, cw_ref):
        h_ref[H0:H0 + C, :] = p_ref[...].astype(F32)
        w = cw_ref[...]
        base = H0 - (CONV_TAPS - 1)
        y = h_ref[base:base + C, :] * w[0:1, :]
        for j in range(1, CONV_TAPS):
            y = y + h_ref[base + j:base + j + C, :] * w[j:j + 1, :]
        h_ref[0:H0, :] = h_ref[C:C + H0, :]
        return y * jax.nn.sigmoid(y)

    yq = conv_silu(pq_ref, hq_ref, cwq_ref)
    yk = conv_silu(pk_ref, hk_ref, cwk_ref)
    yv = conv_silu(pv_ref, hv_ref, cwv_ref)
    z = pz_ref[...].astype(F32)

    row = lax.broadcasted_iota(jnp.int32, (C, C), 0)
    col = lax.broadcasted_iota(jnp.int32, (C, C), 1)
    causal = row >= col
    strict = row > col
    same16 = (row // 16) == (col // 16)
    same32 = (row // 32) == (col // 32)
    tril = jnp.where(causal, 1.0, 0.0).astype(BF16)
    triu = jnp.where(row <= col, 1.0, 0.0).astype(BF16)

    gcol = gcol_ref[...]
    grow = grow_ref[...]
    gh, gm, gl = _split3(gcol[:, hb:])
    gc_col = (jnp.dot(tril, gh, preferred_element_type=F32)
              + jnp.dot(tril, gm, preferred_element_type=F32)
              + jnp.dot(tril, gl, preferred_element_type=F32))
    rh, rm, rl = _split3(grow[hb:, :])
    gc_row = (jnp.dot(rh, triu, preferred_element_type=F32)
              + jnp.dot(rm, triu, preferred_element_type=F32)
              + jnp.dot(rl, triu, preferred_element_type=F32))

    nw = nw_ref[...]
    q_scale = HEAD_W ** -0.5

    for hh in range(hb):
        sl = slice(hh * HEAD_W, (hh + 1) * HEAD_W)
        q = yq[:, sl]
        k = yk[:, sl]
        v = yv[:, sl]
        q = q * (lax.rsqrt(jnp.sum(q * q, axis=-1, keepdims=True) + GDN_EPS) * q_scale)
        k = k * lax.rsqrt(jnp.sum(k * k, axis=-1, keepdims=True) + GDN_EPS)
        beta = gcol[:, hh:hh + 1]
        g_c = gc_col[:, hh:hh + 1]
        g_r = gc_row[hh:hh + 1, :]
        g_last = g_r[:, C - 1:C]
        decay = jnp.exp(jnp.where(causal, g_c - g_r, -jnp.inf))
        eg = jnp.exp(g_c)
        k_beta = k * beta
        kq = _bdot_nt(jnp.concatenate([k_beta, q], axis=0), k)
        low = jnp.where(strict, kq[:C] * decay, 0.0)
        attn = kq[C:] * decay
        n_inv = _unit_lower_inverse_minus_identity(low, same16, same32)
        rhs = jnp.concatenate([v * beta, k_beta * eg], axis=1)
        sol = rhs + _bdot(n_inv, rhs)
        u = sol[:, :HEAD_W]
        w = sol[:, HEAD_W:]
        state = state_ref[hh]
        ws = _bdot(jnp.concatenate([w, q * eg], axis=0), state)
        v_new = u - ws[:C]
        o = ws[C:] + _bdot(attn, v_new)
        k_dec = k * jnp.exp(g_last - g_c)
        state_ref[hh] = state * jnp.exp(g_last) + _bdot_tn(k_dec, v_new)
        o = o * lax.rsqrt(jnp.mean(o * o, axis=-1, keepdims=True) + GDN_EPS) * nw
        zh = z[:, sl]
        o_ref[:, sl] = (o * (zh * jax.nn.sigmoid(zh))).astype(o_ref.dtype)


def _gdn_core(p3, conv_w, gcol, grow, norm_w, hb):
    b, s, w4 = p3.shape
    v_w = w4 // 4
    heads = v_w // HEAD_W
    groups = heads // hb
    wb = hb * HEAD_W
    nch = s // CHUNK

    def pspec(part):
        return pl.BlockSpec((None, CHUNK, wb), lambda bi, gi, ci, part=part: (bi, ci, part * groups + gi))

    def cspec(part):
        return pl.BlockSpec((CONV_TAPS, wb), lambda bi, gi, ci, part=part: (0, part * groups + gi))

    return pl.pallas_call(
        functools.partial(_gdn_kernel, hb=hb),
        grid=(b, groups, nch),
        in_specs=[pspec(0), pspec(1), pspec(2), pspec(3), cspec(0), cspec(1), cspec(2),
                  pl.BlockSpec((None, None, CHUNK, 2 * hb), lambda bi, gi, ci: (bi, gi, ci, 0)),
                  pl.BlockSpec((None, None, None, 2 * hb, CHUNK), lambda bi, gi, ci: (bi, gi, ci, 0, 0)),
                  pl.BlockSpec((1, HEAD_W), lambda bi, gi, ci: (0, 0))],
        out_specs=pl.BlockSpec((None, CHUNK, wb), lambda bi, gi, ci: (bi, ci, gi)),
        out_shape=jax.ShapeDtypeStruct((b, s, v_w), BF16),
        scratch_shapes=[pltpu.VMEM((CHUNK + V7X_SUBLANES, wb), F32),
                        pltpu.VMEM((CHUNK + V7X_SUBLANES, wb), F32),
                        pltpu.VMEM((CHUNK + V7X_SUBLANES, wb), F32),
                        pltpu.VMEM((hb, HEAD_W, HEAD_W), F32)],
        compiler_params=_cparams(("parallel", "parallel", "arbitrary")),
        name="gdn_core",
    )(p3, p3, p3, p3, conv_w, conv_w, conv_w, gcol, grow, norm_w)


def _layer_norm_rows(y, g, b):
    mu = jnp.mean(y, axis=-1, keepdims=True)
    yc = y - mu
    var = jnp.mean(yc * yc, axis=-1, keepdims=True)
    return yc * lax.rsqrt(var + LN_EPS) * g + b


def _proj_ln_kernel(a_ref, w_ref, x_ref, g_ref, b_ref, o_ref, *, alpha):
    h = jnp.dot(a_ref[...].astype(BF16), w_ref[...], preferred_element_type=F32)
    o_ref[...] = _layer_norm_rows(alpha * x_ref[...] + h, g_ref[...], b_ref[...])


def _proj_ln(a, w, x2, g, b, alpha, tm):
    t, k = a.shape
    d = w.shape[1]
    tm = min(tm, t)
    vec = pl.BlockSpec((1, d), lambda i: (0, 0))
    return pl.pallas_call(
        functools.partial(_proj_ln_kernel, alpha=alpha),
        grid=(t // tm,),
        in_specs=[pl.BlockSpec((tm, k), lambda i: (i, 0)),
                  pl.BlockSpec((k, d), lambda i: (0, 0)),
                  pl.BlockSpec((tm, d), lambda i: (i, 0)), vec, vec],
        out_specs=pl.BlockSpec((tm, d), lambda i: (i, 0)),
        out_shape=jax.ShapeDtypeStruct((t, d), F32),
        compiler_params=_cparams(("parallel",)),
        name="proj_ln",
    )(a, w, x2, g, b)


def _mlp_ln_kernel(x_ref, wu_ref, wd_ref, g_ref, b_ref, o_ref, xb_ref, acc_ref, *, alpha):
    f = pl.program_id(1)

    @pl.when(f == 0)
    def _():
        xb_ref[...] = x_ref[...].astype(BF16)
        acc_ref[...] = jnp.zeros_like(acc_ref)

    h = jnp.dot(xb_ref[...], wu_ref[...], preferred_element_type=F32)
    h = jnp.maximum(h, 0.0)
    h = h * h
    acc_ref[...] += jnp.dot(h.astype(BF16), wd_ref[...], preferred_element_type=F32)

    @pl.when(f == pl.num_programs(1) - 1)
    def _():
        o_ref[...] = _layer_norm_rows(alpha * x_ref[...] + acc_ref[...], g_ref[...], b_ref[...])


def _mlp_ln(x2, w_up, w_down, g, b, alpha, tm, tf):
    t, d = x2.shape
    ff = w_up.shape[1]
    tm = min(tm, t)
    tf = min(tf, ff)
    vec = pl.BlockSpec((1, d), lambda i, f: (0, 0))
    return pl.pallas_call(
        functools.partial(_mlp_ln_kernel, alpha=alpha),
        grid=(t // tm, ff // tf),
        in_specs=[pl.BlockSpec((tm, d), lambda i, f: (i, 0)),
                  pl.BlockSpec((d, tf), lambda i, f: (0, f)),
                  pl.BlockSpec((tf, d), lambda i, f: (f, 0)), vec, vec],
        out_specs=pl.BlockSpec((tm, d), lambda i, f: (i, 0)),
        out_shape=jax.ShapeDtypeStruct((t, d), F32),
        scratch_shapes=[pltpu.VMEM((tm, d), BF16), pltpu.VMEM((tm, d), F32)],
        compiler_params=_cparams(("parallel", "arbitrary")),
        name="mlp_ln",
    )(x2, w_up, w_down, g, b)


def _diff_attn_kernel(q_ref, k_ref, v_ref, lam_ref, sw_ref, o_ref, qs_ref, m_ref, l_ref, acc_ref,
                      *, tq, lambda_init):
    i = pl.program_id(2)
    half = HEAD_W // 2
    lane = lax.broadcasted_iota(jnp.int32, (tq, HEAD_W), 1)
    q = q_ref[...]
    zero = jnp.zeros_like(q)
    qs_ref[0:tq, :] = jnp.where(lane < half, q, zero)
    qs_ref[tq:2 * tq, :] = jnp.where(lane >= half, q, zero)
    m_ref[...] = jnp.full_like(m_ref, -jnp.inf)
    l_ref[...] = jnp.zeros_like(l_ref)
    acc_ref[...] = jnp.zeros_like(acc_ref)

    def step(j, masked):
        start = pl.multiple_of(j * tq, tq)
        kb = k_ref[pl.ds(start, tq), :]
        vb = v_ref[pl.ds(start, tq), :]
        s = lax.dot_general(qs_ref[...], kb, (((1,), (1,)), ((), ())), preferred_element_type=F32)
        if masked:
            r = lax.broadcasted_iota(jnp.int32, (2 * tq, tq), 0)
            c = lax.broadcasted_iota(jnp.int32, (2 * tq, tq), 1)
            r = jnp.where(r >= tq, r - tq, r)
            s = jnp.where(c <= r, s, -jnp.inf)
        m_prev = m_ref[...]
        m_new = jnp.maximum(m_prev, jnp.max(s, axis=-1, keepdims=True))
        alpha = jnp.exp(m_prev - m_new)
        p = jnp.exp(s - m_new)
        l_ref[...] = alpha * l_ref[...] + jnp.sum(p, axis=-1, keepdims=True)
        acc_ref[...] = alpha * acc_ref[...] + jnp.dot(p.astype(BF16), vb, preferred_element_type=F32)
        m_ref[...] = m_new

    def body(j, carry):
        step(j, False)
        return carry

    lax.fori_loop(0, i, body, 0)
    step(i, True)

    lp = lam_ref[...]
    lam = (jnp.exp(jnp.sum(lp[0:1, :] * lp[1:2, :], axis=-1, keepdims=True))
           - jnp.exp(jnp.sum(lp[2:3, :] * lp[3:4, :], axis=-1, keepdims=True)) + lambda_init)
    o_all = acc_ref[...] / l_ref[...]
    o = o_all[0:tq, :] - lam * o_all[tq:2 * tq, :]
    o = o * lax.rsqrt(jnp.mean(o * o, axis=-1, keepdims=True) + SUBLN_EPS) * sw_ref[...]
    o_ref[...] = (o * (1.0 - lambda_init)).astype(o_ref.dtype)


def _diff_attention(q3, kv3, lam_params, subln_w, lambda_init, tq):
    b, s, wq = q3.shape
    heads = wq // HEAD_W
    tq = min(tq, s)
    return pl.pallas_call(
        functools.partial(_diff_attn_kernel, tq=tq, lambda_init=lambda_init),
        grid=(b, heads, s // tq),
        in_specs=[pl.BlockSpec((None, tq, HEAD_W), lambda bi, h, i: (bi, i, h)),
                  pl.BlockSpec((None, s, HEAD_W), lambda bi, h, i: (bi, 0, h)),
                  pl.BlockSpec((None, s, HEAD_W), lambda bi, h, i, heads=heads: (bi, 0, heads + h)),
                  pl.BlockSpec(lam_params.shape, lambda bi, h, i: (0, 0)),
                  pl.BlockSpec((1, HEAD_W), lambda bi, h, i: (0, 0))],
        out_specs=pl.BlockSpec((None, tq, HEAD_W), lambda bi, h, i: (bi, i, h)),
        out_shape=jax.ShapeDtypeStruct((b, s, wq), BF16),
        scratch_shapes=[pltpu.VMEM((2 * tq, HEAD_W), BF16),
                        pltpu.VMEM((2 * tq, 1), F32),
                        pltpu.VMEM((2 * tq, 1), F32),
                        pltpu.VMEM((2 * tq, HEAD_W), F32)],
        compiler_params=_cparams(("parallel", "parallel", "arbitrary")),
        name="diff_attn",
    )(q3, kv3, kv3, lam_params, subln_w)


def _gdn_layer(x2, bsz, seq, w_in, conv_w, a_log, dt_bias, norm_w, w_out, ln_g, ln_b, alpha, hb, tiles):
    t, d = x2.shape
    heads = a_log.shape[0]
    v_w = heads * HEAD_W
    groups = heads // hb
    main = 4 * v_w
    p = _matmul(x2, w_in[:, :main].astype(BF16), F32, tiles["tm"], tiles["tn"], name="gdn_in_proj")
    wb = w_in[:, main:main + heads].reshape(d, groups, hb)
    wa = w_in[:, main + heads:main + 2 * heads].reshape(d, groups, hb)
    w_gate = jnp.concatenate([wb, wa], axis=2).reshape(d, 2 * heads)
    w_gate = jnp.pad(w_gate, ((0, 0), (0, GATE_W - 2 * heads)))

    def per_col(vals, fill):
        zeros = jnp.full((groups, hb), fill, F32)
        cols = jnp.concatenate([zeros, vals.astype(F32).reshape(groups, hb)], axis=1).reshape(1, 2 * heads)
        return jnp.pad(cols, ((0, 0), (0, GATE_W - 2 * heads)), constant_values=fill)

    is_a = per_col(jnp.ones((heads,), F32), 0.0)
    neg_a = per_col(-jnp.exp(a_log.astype(F32)), 0.0)
    dtb = per_col(dt_bias, 0.0)
    gates = _gates(x2, w_gate, is_a, neg_a, dtb, tiles["tm_gate"])[:, :2 * heads]
    gcol = gates.reshape(bsz, seq, groups, 2 * hb).transpose(0, 2, 1, 3)
    grow = gates.reshape(bsz, seq // CHUNK, CHUNK, groups, 2 * hb).transpose(0, 3, 1, 4, 2)
    o = _gdn_core(p.reshape(bsz, seq, main), conv_w, gcol, grow, norm_w.reshape(1, HEAD_W), hb)
    return _proj_ln(o.reshape(t, v_w), w_out.astype(BF16), x2, ln_g.reshape(1, d), ln_b.reshape(1, d),
                    alpha, tiles["tm_proj"])


def _tiles():
    return dict(tm=1024, tn=1024, tm_gate=512, tm_proj=512, tm_mlp=512, tf_mlp=512, tq=256, hb=8)


def _forward(x, gdn_w_in, gdn_conv_w, gdn_a_log, gdn_dt_bias, gdn_norm_w, gdn_w_out, diff_w_q, diff_lambda,
             diff_subln_w, diff_w_o, shared_w_kv, mlp_w_up, mlp_w_down, ln_g, ln_b, tiles):
    bsz, seq, d = x.shape
    t = bsz * seq
    depth = mlp_w_up.shape[0]
    n_a = gdn_w_in.shape[0]
    alpha = (2 * depth) ** 0.25
    x2 = x.reshape(t, d).astype(F32)
    kv3 = None
    for l in range(depth):
        if l < n_a:
            x2 = _gdn_layer(x2, bsz, seq, gdn_w_in[l], gdn_conv_w[l], gdn_a_log[l], gdn_dt_bias[l],
                            gdn_norm_w[l], gdn_w_out[l], ln_g[l, 0], ln_b[l, 0], alpha,
                            min(tiles["hb"], gdn_a_log.shape[1]), tiles)
        else:
            j = l - n_a
            lambda_init = 0.8 - 0.6 * math.exp(-0.3 * l)
            dk = diff_lambda.shape[-1]
            q = _matmul(x2, diff_w_q[j].astype(BF16), BF16, tiles["tm"], tiles["tn"], scale=dk ** -0.5,
                        name="diff_q_proj")
            o = _diff_attention(q.reshape(bsz, seq, -1), kv3, diff_lambda[j].astype(F32),
                                diff_subln_w[j].reshape(1, HEAD_W).astype(F32), lambda_init, tiles["tq"])
            x2 = _proj_ln(o.reshape(t, -1), diff_w_o[j].astype(BF16), x2, ln_g[l, 0].reshape(1, d),
                          ln_b[l, 0].reshape(1, d), alpha, tiles["tm_proj"])
        x2 = _mlp_ln(x2, mlp_w_up[l].astype(BF16), mlp_w_down[l].astype(BF16), ln_g[l, 1].reshape(1, d),
                     ln_b[l, 1].reshape(1, d), alpha, tiles["tm_mlp"], tiles["tf_mlp"])
        if l == n_a - 1:
            kv = _matmul(x2, shared_w_kv.astype(BF16), BF16, tiles["tm"], tiles["tn"], name="shared_kv_proj")
            kv3 = kv.reshape(bsz, seq, -1)
    return x2.reshape(bsz, seq, d).astype(x.dtype)


def kernel(x, gdn_w_in, gdn_conv_w, gdn_a_log, gdn_dt_bias, gdn_norm_w, gdn_w_out, diff_w_q, diff_lambda,
           diff_subln_w, diff_w_o, shared_w_kv, mlp_w_up, mlp_w_down, ln_g, ln_b):
    return _forward(x, gdn_w_in, gdn_conv_w, gdn_a_log, gdn_dt_bias, gdn_norm_w, gdn_w_out, diff_w_q,
                    diff_lambda, diff_subln_w, diff_w_o, shared_w_kv, mlp_w_up, mlp_w_down, ln_g, ln_b,
                    _tiles())
```

```python
import functools
import math

import jax
import jax.numpy as jnp
from jax import lax
from jax.experimental import pallas as pl
from jax.experimental.pallas import tpu as pltpu

F32 = jnp.float32
BF16 = jnp.bfloat16

HEAD_W = 128
CHUNK = 64
CONV_TAPS = 4
GDN_EPS = 1e-6
SUBLN_EPS = 1e-5
LN_EPS = 1e-5
GATE_W = 128

V7X_SUBLANES = 8
V7X_VMEM_LIMIT_BYTES = 56 * 1024 * 1024


def _cparams(sem):
    return pltpu.CompilerParams(dimension_semantics=sem, vmem_limit_bytes=V7X_VMEM_LIMIT_BYTES)


def _bdot(a, b):
    return jnp.dot(a.astype(BF16), b.astype(BF16), preferred_element_type=F32)


def _bdot_nt(a, b):
    return lax.dot_general(a.astype(BF16), b.astype(BF16), (((1,), (1,)), ((), ())),
                           preferred_element_type=F32)


def _bdot_tn(a, b):
    return lax.dot_general(a.astype(BF16), b.astype(BF16), (((0,), (0,)), ((), ())),
                           preferred_element_type=F32)


def _split3(x):
    h = x.astype(BF16)
    r = x - h.astype(F32)
    m = r.astype(BF16)
    l = (r - m.astype(F32)).astype(BF16)
    return h, m, l


def _mm_kernel(x_ref, w_ref, o_ref, *, scale):
    acc = jnp.dot(x_ref[...].astype(BF16), w_ref[...], preferred_element_type=F32)
    if scale != 1.0:
        acc = acc * scale
    o_ref[...] = acc.astype(o_ref.dtype)


def _matmul(x, w, out_dtype, tm, tn, scale=1.0, name="matmul"):
    m, k = x.shape
    n = w.shape[1]
    tm = min(tm, m)
    tn = min(tn, n)
    return pl.pallas_call(
        functools.partial(_mm_kernel, scale=scale),
        grid=(m // tm, n // tn),
        in_specs=[pl.BlockSpec((tm, k), lambda i, j: (i, 0)),
                  pl.BlockSpec((k, tn), lambda i, j: (0, j))],
        out_specs=pl.BlockSpec((tm, tn), lambda i, j: (i, j)),
        out_shape=jax.ShapeDtypeStruct((m, n), out_dtype),
        compiler_params=_cparams(("parallel", "parallel")),
        name=name,
    )(x, w)


def _gate_kernel(x_ref, w_ref, isa_ref, nega_ref, dtb_ref, o_ref):
    x = x_ref[...]
    w = w_ref[...]
    xh = x.astype(BF16)
    xl = (x - xh.astype(F32)).astype(BF16)
    wh = w.astype(BF16)
    wl = (w - wh.astype(F32)).astype(BF16)
    acc = (jnp.dot(xh, wh, preferred_element_type=F32)
           + jnp.dot(xl, wh, preferred_element_type=F32)
           + jnp.dot(xh, wl, preferred_element_type=F32))
    beta = jax.nn.sigmoid(acc)
    s = acc + dtb_ref[...]
    softplus = jnp.maximum(s, 0.0) + jnp.log1p(jnp.exp(-jnp.abs(s)))
    g = nega_ref[...] * softplus
    o_ref[...] = jnp.where(isa_ref[...] > 0.5, g, beta)


def _gates(x2, w_gate, is_a, neg_a, dtb, tm):
    t, d = x2.shape
    tm = min(tm, t)
    vec = pl.BlockSpec((1, GATE_W), lambda i: (0, 0))
    return pl.pallas_call(
        _gate_kernel,
        grid=(t // tm,),
        in_specs=[pl.BlockSpec((tm, d), lambda i: (i, 0)),
                  pl.BlockSpec((d, GATE_W), lambda i: (0, 0)), vec, vec, vec],
        out_specs=pl.BlockSpec((tm, GATE_W), lambda i: (i, 0)),
        out_shape=jax.ShapeDtypeStruct((t, GATE_W), F32),
        compiler_params=_cparams(("parallel",)),
        name="gdn_gates",
    )(x2, w_gate, is_a, neg_a, dtb)


def _each(fn, *lists):
    return [fn(*args) for args in zip(*lists)]


def _unit_lower_inverse_minus_identity(lows, same16, same32):
    c = lows[0].shape[0]
    d = _each(lambda low: jnp.where(same16, low, 0.0), lows)
    x = _each(lambda a: -a, d)
    p = _each(lambda a: _bdot(a, a), d)
    xp = _each(lambda a, b: _bdot(jnp.concatenate([a, b], axis=0), b), x, p)
    n = _each(lambda a, b, ab: a + b + ab[:c], x, p, xp)
    p2 = _each(lambda ab: ab[c:], xp)
    np2 = _each(lambda a, b: _bdot(jnp.concatenate([a, b], axis=0), b), n, p2)
    n = _each(lambda a, b, ab: a + b + ab[:c], n, p2, np2)
    p4 = _each(lambda ab: ab[c:], np2)
    n = _each(lambda a, b: a + b + _bdot(a, b), n, p4)
    only32 = jnp.logical_and(same32, jnp.logical_not(same16))
    off1 = _each(lambda low: jnp.where(only32, low, 0.0), lows)
    c1 = _each(lambda a, o: o + _bdot(a, o), n, off1)
    n = _each(lambda a, cc: a - (cc + _bdot(cc, a)), n, c1)
    off2 = _each(lambda low: jnp.where(same32, 0.0, low), lows)
    c2 = _each(lambda a, o: o + _bdot(a, o), n, off2)
    n = _each(lambda a, cc: a - (cc + _bdot(cc, a)), n, c2)
    return n


def _gdn_kernel(pq_ref, pk_ref, pv_ref, pz_ref, cwq_ref, cwk_ref, cwv_ref, gcol_ref, grow_ref,
                nw_ref, o_ref, hq_ref, hk_ref, hv_ref, state_ref, *, hb):
    c_idx = pl.program_id(2)
    C = CHUNK
    H0 = V7X_SUBLANES
    w_blk = hb * HEAD_W

    @pl.when(c_idx == 0)
    def _():
        state_ref[...] = jnp.zeros_like(state_ref)
        zero = jnp.zeros((H0, w_blk), F32)
        hq_ref[0:H0, :] = zero
        hk_ref[0:H0, :] = zero
        hv_ref[0:H0, :] = zero

    def conv_silu(p_ref, h_ref, cw_ref):
        h_ref[H0:H0 + C, :] = p_ref[...].astype(F32)
        e = h_ref[...]
        w = cw_ref[...]
        t = e * w[0:1, :]
        for j in range(1, CONV_TAPS):
            t = pltpu.roll(t, 1, axis=0) + e * w[j:j + 1, :]
        h_ref[0:H0, :] = e[C:C + H0, :]
        y = t[H0:H0 + C, :]
        return y * jax.nn.sigmoid(y)

    yq = conv_silu(pq_ref, hq_ref, cwq_ref)
    yk = conv_silu(pk_ref, hk_ref, cwk_ref)
    yv = conv_silu(pv_ref, hv_ref, cwv_ref)
    z = pz_ref[...].astype(F32)

    row = lax.broadcasted_iota(jnp.int32, (C, C), 0)
    col = lax.broadcasted_iota(jnp.int32, (C, C), 1)
    causal = row >= col
    strict = row > col
    same16 = (row // 16) == (col // 16)
    same32 = (row // 32) == (col // 32)
    tril = jnp.where(causal, 1.0, 0.0).astype(BF16)
    triu = jnp.where(row <= col, 1.0, 0.0).astype(BF16)

    gcol = gcol_ref[...]
    grow = grow_ref[...]
    gh, gm, gl = _split3(gcol[:, hb:])
    gc_col = (jnp.dot(tril, gh, preferred_element_type=F32)
              + jnp.dot(tril, gm, preferred_element_type=F32)
              + jnp.dot(tril, gl, preferred_element_type=F32))
    rh, rm, rl = _split3(grow[hb:, :])
    gc_row = (jnp.dot(rh, triu, preferred_element_type=F32)
              + jnp.dot(rm, triu, preferred_element_type=F32)
              + jnp.dot(rl, triu, preferred_element_type=F32))

    nw = nw_ref[...]
    q_scale = HEAD_W ** -0.5

    heads = list(range(hb))
    sl = [slice(h * HEAD_W, (h + 1) * HEAD_W) for h in heads]

    def l2n(y, scale):
        return y * (lax.rsqrt(jnp.sum(y * y, axis=-1, keepdims=True) + GDN_EPS) * scale)

    q = [l2n(yq[:, s], q_scale) for s in sl]
    k = [l2n(yk[:, s], 1.0) for s in sl]
    v = [yv[:, s] for s in sl]
    beta = [gcol[:, h:h + 1] for h in heads]
    g_c = [gc_col[:, h:h + 1] for h in heads]
    g_r = [gc_row[h:h + 1, :] for h in heads]
    g_last = [r[:, C - 1:C] for r in g_r]
    decay = _each(lambda gc, gr: jnp.exp(jnp.where(causal, gc - gr, -jnp.inf)), g_c, g_r)
    eg = _each(jnp.exp, g_c)
    k_beta = _each(lambda a, b: a * b, k, beta)
    kq = _each(lambda kb, qq, kk: _bdot_nt(jnp.concatenate([kb, qq], axis=0), kk), k_beta, q, k)
    low = _each(lambda a, dc: jnp.where(strict, a[:C] * dc, 0.0), kq, decay)
    attn = _each(lambda a, dc: a[C:] * dc, kq, decay)
    n_inv = _unit_lower_inverse_minus_identity(low, same16, same32)
    rhs = _each(lambda vv, b, kb, e: jnp.concatenate([vv * b, kb * e], axis=1), v, beta, k_beta, eg)
    sol = _each(lambda r, n: r + _bdot(n, r), rhs, n_inv)
    state = [state_ref[h] for h in heads]
    ws = _each(lambda s, qq, e, st: _bdot(jnp.concatenate([s[:, HEAD_W:], qq * e], axis=0), st),
               sol, q, eg, state)
    v_new = _each(lambda s, a: s[:, :HEAD_W] - a[:C], sol, ws)
    o = _each(lambda a, at, vn: a[C:] + _bdot(at, vn), ws, attn, v_new)
    k_dec = _each(lambda kk, gl_, gc: kk * jnp.exp(gl_ - gc), k, g_last, g_c)
    new_state = _each(lambda st, gl_, kd, vn: st * jnp.exp(gl_) + _bdot_tn(kd, vn),
                      state, g_last, k_dec, v_new)
    for h in heads:
        state_ref[h] = new_state[h]
    o = _each(lambda a: a * lax.rsqrt(jnp.mean(a * a, axis=-1, keepdims=True) + GDN_EPS) * nw, o)
    for h in heads:
        zh = z[:, sl[h]]
        o_ref[:, sl[h]] = (o[h] * (zh * jax.nn.sigmoid(zh))).astype(o_ref.dtype)


def _gdn_core(p3, conv_w, gcol, grow, norm_w, hb):
    b, s, w4 = p3.shape
    v_w = w4 // 4
    heads = v_w // HEAD_W
    groups = heads // hb
    wb = hb * HEAD_W
    nch = s // CHUNK

    def pspec(part):
        return pl.BlockSpec((None, CHUNK, wb), lambda bi, gi, ci, part=part: (bi, ci, part * groups + gi))

    def cspec(part):
        return pl.BlockSpec((CONV_TAPS, wb), lambda bi, gi, ci, part=part: (0, part * groups + gi))

    return pl.pallas_call(
        functools.partial(_gdn_kernel, hb=hb),
        grid=(b, groups, nch),
        in_specs=[pspec(0), pspec(1), pspec(2), pspec(3), cspec(0), cspec(1), cspec(2),
                  pl.BlockSpec((None, None, CHUNK, 2 * hb), lambda bi, gi, ci: (bi, gi, ci, 0)),
                  pl.BlockSpec((None, None, None, 2 * hb, CHUNK), lambda bi, gi, ci: (bi, gi, ci, 0, 0)),
                  pl.BlockSpec((1, HEAD_W), lambda bi, gi, ci: (0, 0))],
        out_specs=pl.BlockSpec((None, CHUNK, wb), lambda bi, gi, ci: (bi, ci, gi)),
        out_shape=jax.ShapeDtypeStruct((b, s, v_w), BF16),
        scratch_shapes=[pltpu.VMEM((CHUNK + V7X_SUBLANES, wb), F32),
                        pltpu.VMEM((CHUNK + V7X_SUBLANES, wb), F32),
                        pltpu.VMEM((CHUNK + V7X_SUBLANES, wb), F32),
                        pltpu.VMEM((hb, HEAD_W, HEAD_W), F32)],
        compiler_params=_cparams(("parallel", "parallel", "arbitrary")),
        name="gdn_core",
    )(p3, p3, p3, p3, conv_w, conv_w, conv_w, gcol, grow, norm_w)


def _layer_norm_rows(y, g, b):
    mu = jnp.mean(y, axis=-1, keepdims=True)
    yc = y - mu
    var = jnp.mean(yc * yc, axis=-1, keepdims=True)
    return yc * lax.rsqrt(var + LN_EPS) * g + b


def _proj_ln_kernel(a_ref, w_ref, x_ref, g_ref, b_ref, o_ref, *, alpha):
    h = jnp.dot(a_ref[...].astype(BF16), w_ref[...], preferred_element_type=F32)
    o_ref[...] = _layer_norm_rows(alpha * x_ref[...] + h, g_ref[...], b_ref[...])


def _proj_ln(a, w, x2, g, b, alpha, tm):
    t, k = a.shape
    d = w.shape[1]
    tm = min(tm, t)
    vec = pl.BlockSpec((1, d), lambda i: (0, 0))
    return pl.pallas_call(
        functools.partial(_proj_ln_kernel, alpha=alpha),
        grid=(t // tm,),
        in_specs=[pl.BlockSpec((tm, k), lambda i: (i, 0)),
                  pl.BlockSpec((k, d), lambda i: (0, 0)),
                  pl.BlockSpec((tm, d), lambda i: (i, 0)), vec, vec],
        out_specs=pl.BlockSpec((tm, d), lambda i: (i, 0)),
        out_shape=jax.ShapeDtypeStruct((t, d), F32),
        compiler_params=_cparams(("parallel",)),
        name="proj_ln",
    )(a, w, x2, g, b)


def _mlp_ln_kernel(x_ref, wu_ref, wd_ref, g_ref, b_ref, o_ref, xb_ref, acc_ref, *, alpha):
    f = pl.program_id(1)

    @pl.when(f == 0)
    def _():
        xb_ref[...] = x_ref[...].astype(BF16)
        acc_ref[...] = jnp.zeros_like(acc_ref)

    h = jnp.dot(xb_ref[...], wu_ref[...], preferred_element_type=F32)
    h = jnp.maximum(h, 0.0)
    h = h * h
    acc_ref[...] += jnp.dot(h.astype(BF16), wd_ref[...], preferred_element_type=F32)

    @pl.when(f == pl.num_programs(1) - 1)
    def _():
        o_ref[...] = _layer_norm_rows(alpha * x_ref[...] + acc_ref[...], g_ref[...], b_ref[...])


def _mlp_ln(x2, w_up, w_down, g, b, alpha, tm, tf):
    t, d = x2.shape
    ff = w_up.shape[1]
    tm = min(tm, t)
    tf = min(tf, ff)
    vec = pl.BlockSpec((1, d), lambda i, f: (0, 0))
    return pl.pallas_call(
        functools.partial(_mlp_ln_kernel, alpha=alpha),
        grid=(t // tm, ff // tf),
        in_specs=[pl.BlockSpec((tm, d), lambda i, f: (i, 0)),
                  pl.BlockSpec((d, tf), lambda i, f: (0, f)),
                  pl.BlockSpec((tf, d), lambda i, f: (f, 0)), vec, vec],
        out_specs=pl.BlockSpec((tm, d), lambda i, f: (i, 0)),
        out_shape=jax.ShapeDtypeStruct((t, d), F32),
        scratch_shapes=[pltpu.VMEM((tm, d), BF16), pltpu.VMEM((tm, d), F32)],
        compiler_params=_cparams(("parallel", "arbitrary")),
        name="mlp_ln",
    )(x2, w_up, w_down, g, b)


def _diff_attn_kernel(q_ref, k_ref, v_ref, lam_ref, sw_ref, o_ref, qs_ref, acc_ref,
                      *, tq, tk, lambda_init):
    i = pl.program_id(2)
    half = HEAD_W // 2
    lane = lax.broadcasted_iota(jnp.int32, (tq, HEAD_W), 1)
    q = q_ref[...]
    zero = jnp.zeros_like(q)
    qs_ref[0:tq, :] = jnp.where(lane < half, q, zero)
    qs_ref[tq:2 * tq, :] = jnp.where(lane >= half, q, zero)
    acc_ref[...] = jnp.zeros_like(acc_ref)

    def step(j, m_prev, l_prev, masked):
        start = pl.multiple_of(j * tk, tk)
        kb = k_ref[pl.ds(start, tk), :]
        vb = v_ref[pl.ds(start, tk), :]
        s = lax.dot_general(kb, qs_ref[...], (((1,), (1,)), ((), ())), preferred_element_type=F32)
        if masked:
            key = lax.broadcasted_iota(jnp.int32, (tk, 2 * tq), 0) + j * tk
            qry = lax.broadcasted_iota(jnp.int32, (tk, 2 * tq), 1)
            qry = jnp.where(qry >= tq, qry - tq, qry) + i * tq
            s = jnp.where(key <= qry, s, -jnp.inf)
        m_new = jnp.maximum(m_prev, jnp.max(s, axis=0, keepdims=True))
        alpha = jnp.exp2(m_prev - m_new)
        p = jnp.exp2(s - m_new)
        l_new = alpha * l_prev + jnp.sum(p, axis=0, keepdims=True)
        pv = lax.dot_general(vb, p.astype(BF16), (((0,), (0,)), ((), ())), preferred_element_type=F32)
        acc_ref[...] = alpha * acc_ref[...] + pv
        return m_new, l_new

    n_diag = tq // tk
    first_diag = i * n_diag
    m0 = jnp.full((1, 2 * tq), -jnp.inf, F32)
    l0 = jnp.zeros((1, 2 * tq), F32)
    m, l = lax.fori_loop(0, first_diag, lambda j, c: step(j, c[0], c[1], False), (m0, l0))
    for d in range(n_diag):
        m, l = step(first_diag + d, m, l, True)

    lp = lam_ref[...]
    lam = (jnp.exp(jnp.sum(lp[0:1, :] * lp[1:2, :], axis=-1, keepdims=True))
           - jnp.exp(jnp.sum(lp[2:3, :] * lp[3:4, :], axis=-1, keepdims=True)) + lambda_init)
    o_all = acc_ref[...] * (1.0 / l)
    o = (o_all[:, 0:tq] - lam * o_all[:, tq:2 * tq]).T
    o = o * lax.rsqrt(jnp.mean(o * o, axis=-1, keepdims=True) + SUBLN_EPS) * sw_ref[...]
    o_ref[...] = (o * (1.0 - lambda_init)).astype(o_ref.dtype)


def _diff_attention(q3, kv3, lam_params, subln_w, lambda_init, tq, tk):
    b, s, wq = q3.shape
    heads = wq // HEAD_W
    tq = min(tq, s)
    tk = min(tk, tq)
    return pl.pallas_call(
        functools.partial(_diff_attn_kernel, tq=tq, tk=tk, lambda_init=lambda_init),
        grid=(b, heads, s // tq),
        in_specs=[pl.BlockSpec((None, tq, HEAD_W), lambda bi, h, i: (bi, i, h)),
                  pl.BlockSpec((None, s, HEAD_W), lambda bi, h, i: (bi, 0, h)),
                  pl.BlockSpec((None, s, HEAD_W), lambda bi, h, i, heads=heads: (bi, 0, heads + h)),
                  pl.BlockSpec(lam_params.shape, lambda bi, h, i: (0, 0)),
                  pl.BlockSpec((1, HEAD_W), lambda bi, h, i: (0, 0))],
        out_specs=pl.BlockSpec((None, tq, HEAD_W), lambda bi, h, i: (bi, i, h)),
        out_shape=jax.ShapeDtypeStruct((b, s, wq), BF16),
        scratch_shapes=[pltpu.VMEM((2 * tq, HEAD_W), BF16),
                        pltpu.VMEM((HEAD_W, 2 * tq), F32)],
        compiler_params=_cparams(("parallel", "parallel", "arbitrary")),
        name="diff_attn",
    )(q3, kv3, kv3, lam_params, subln_w)


def _gdn_layer(x2, bsz, seq, w_in, conv_w, a_log, dt_bias, norm_w, w_out, ln_g, ln_b, alpha, hb, tiles):
    t, d = x2.shape
    heads = a_log.shape[0]
    v_w = heads * HEAD_W
    groups = heads // hb
    main = 4 * v_w
    p = _matmul(x2, w_in[:, :main].astype(BF16), BF16, tiles["tm"], tiles["tn"], name="gdn_in_proj")
    wb = w_in[:, main:main + heads].reshape(d, groups, hb)
    wa = w_in[:, main + heads:main + 2 * heads].reshape(d, groups, hb)
    w_gate = jnp.concatenate([wb, wa], axis=2).reshape(d, 2 * heads)
    w_gate = jnp.pad(w_gate, ((0, 0), (0, GATE_W - 2 * heads)))

    def per_col(vals, fill):
        zeros = jnp.full((groups, hb), fill, F32)
        cols = jnp.concatenate([zeros, vals.astype(F32).reshape(groups, hb)], axis=1).reshape(1, 2 * heads)
        return jnp.pad(cols, ((0, 0), (0, GATE_W - 2 * heads)), constant_values=fill)

    is_a = per_col(jnp.ones((heads,), F32), 0.0)
    neg_a = per_col(-jnp.exp(a_log.astype(F32)), 0.0)
    dtb = per_col(dt_bias, 0.0)
    gates = _gates(x2, w_gate, is_a, neg_a, dtb, tiles["tm_gate"])[:, :2 * heads]
    gcol = gates.reshape(bsz, seq, groups, 2 * hb).transpose(0, 2, 1, 3)
    grow = gates.reshape(bsz, seq // CHUNK, CHUNK, groups, 2 * hb).transpose(0, 3, 1, 4, 2)
    o = _gdn_core(p.reshape(bsz, seq, main), conv_w, gcol, grow, norm_w.reshape(1, HEAD_W), hb)
    return _proj_ln(o.reshape(t, v_w), w_out.astype(BF16), x2, ln_g.reshape(1, d), ln_b.reshape(1, d),
                    alpha, tiles["tm_proj"])


def _tiles():
    return dict(tm=1024, tn=1024, tm_gate=512, tm_proj=512, tm_mlp=512, tf_mlp=512, tq=512, tk=256, hb=16)


def _forward(x, gdn_w_in, gdn_conv_w, gdn_a_log, gdn_dt_bias, gdn_norm_w, gdn_w_out, diff_w_q, diff_lambda,
             diff_subln_w, diff_w_o, shared_w_kv, mlp_w_up, mlp_w_down, ln_g, ln_b, tiles):
    bsz, seq, d = x.shape
    t = bsz * seq
    depth = mlp_w_up.shape[0]
    n_a = gdn_w_in.shape[0]
    alpha = (2 * depth) ** 0.25
    x2 = x.reshape(t, d).astype(F32)
    kv3 = None
    for l in range(depth):
        if l < n_a:
            x2 = _gdn_layer(x2, bsz, seq, gdn_w_in[l], gdn_conv_w[l], gdn_a_log[l], gdn_dt_bias[l],
                            gdn_norm_w[l], gdn_w_out[l], ln_g[l, 0], ln_b[l, 0], alpha,
                            min(tiles["hb"], gdn_a_log.shape[1]), tiles)
        else:
            j = l - n_a
            lambda_init = 0.8 - 0.6 * math.exp(-0.3 * l)
            dk = diff_lambda.shape[-1]
            q = _matmul(x2, diff_w_q[j].astype(BF16), BF16, tiles["tm"], tiles["tn"],
                        scale=dk ** -0.5 * math.log2(math.e), name="diff_q_proj")
            o = _diff_attention(q.reshape(bsz, seq, -1), kv3, diff_lambda[j].astype(F32),
                                diff_subln_w[j].reshape(1, HEAD_W).astype(F32), lambda_init, tiles["tq"],
                                tiles["tk"])
            x2 = _proj_ln(o.reshape(t, -1), diff_w_o[j].astype(BF16), x2, ln_g[l, 0].reshape(1, d),
                          ln_b[l, 0].reshape(1, d), alpha, tiles["tm_proj"])
        x2 = _mlp_ln(x2, mlp_w_up[l].astype(BF16), mlp_w_down[l].astype(BF16), ln_g[l, 1].reshape(1, d),
                     ln_b[l, 1].reshape(1, d), alpha, tiles["tm_mlp"], tiles["tf_mlp"])
        if l == n_a - 1:
            kv = _matmul(x2, shared_w_kv.astype(BF16), BF16, tiles["tm"], tiles["tn"], name="shared_kv_proj")
            kv3 = kv.reshape(bsz, seq, -1)
    return x2.reshape(bsz, seq, d).astype(x.dtype)


def kernel(x, gdn_w_in, gdn_conv_w, gdn_a_log, gdn_dt_bias, gdn_norm_w, gdn_w_out, diff_w_q, diff_lambda,
           diff_subln_w, diff_w_o, shared_w_kv, mlp_w_up, mlp_w_down, ln_g, ln_b):
    return _forward(x, gdn_w_in, gdn_conv_w, gdn_a_log, gdn_dt_bias, gdn_norm_w, gdn_w_out, diff_w_q,
                    diff_lambda, diff_subln_w, diff_w_o, shared_w_kv, mlp_w_up, mlp_w_down, ln_g, ln_b,
                    _tiles())
```

```python
import functools
import math

import jax
import jax.numpy as jnp
from jax import lax
from jax.experimental import pallas as pl
from jax.experimental.pallas import tpu as pltpu

F32 = jnp.float32
BF16 = jnp.bfloat16

HEAD_W = 128
CHUNK = 64
CONV_TAPS = 4
GDN_EPS = 1e-6
SUBLN_EPS = 1e-5
LN_EPS = 1e-5
GATE_W = 128

V7X_SUBLANES = 8
V7X_MXU_COLS = 256
ONES_ROWS = 16
V7X_VMEM_LIMIT_BYTES = 56 * 1024 * 1024


def _cparams(sem):
    return pltpu.CompilerParams(dimension_semantics=sem, vmem_limit_bytes=V7X_VMEM_LIMIT_BYTES)


def _bdot(a, b):
    return jnp.dot(a.astype(BF16), b.astype(BF16), preferred_element_type=F32)


def _bdot_nt(a, b):
    return lax.dot_general(a.astype(BF16), b.astype(BF16), (((1,), (1,)), ((), ())),
                           preferred_element_type=F32)


def _bdot_tn(a, b):
    return lax.dot_general(a.astype(BF16), b.astype(BF16), (((0,), (0,)), ((), ())),
                           preferred_element_type=F32)


def _split3(x):
    h = x.astype(BF16)
    r = x - h.astype(F32)
    m = r.astype(BF16)
    l = (r - m.astype(F32)).astype(BF16)
    return h, m, l


def _mm_kernel(x_ref, w_ref, o_ref, *, scale):
    acc = jnp.dot(x_ref[...].astype(BF16), w_ref[...], preferred_element_type=F32)
    if scale != 1.0:
        acc = acc * scale
    o_ref[...] = acc.astype(o_ref.dtype)


def _matmul(x, w, out_dtype, tm, tn, scale=1.0, name="matmul"):
    m, k = x.shape
    n = w.shape[1]
    tm = min(tm, m)
    tn = min(tn, n)
    return pl.pallas_call(
        functools.partial(_mm_kernel, scale=scale),
        grid=(m // tm, n // tn),
        in_specs=[pl.BlockSpec((tm, k), lambda i, j: (i, 0)),
                  pl.BlockSpec((k, tn), lambda i, j: (0, j))],
        out_specs=pl.BlockSpec((tm, tn), lambda i, j: (i, j)),
        out_shape=jax.ShapeDtypeStruct((m, n), out_dtype),
        compiler_params=_cparams(("parallel", "parallel")),
        name=name,
    )(x, w)


def _gate_kernel(x_ref, w_ref, isa_ref, nega_ref, dtb_ref, o_ref):
    x = x_ref[...]
    w = w_ref[...]
    xh = x.astype(BF16)
    xl = (x - xh.astype(F32)).astype(BF16)
    wh = w.astype(BF16)
    wl = (w - wh.astype(F32)).astype(BF16)
    acc = (jnp.dot(xh, wh, preferred_element_type=F32)
           + jnp.dot(xl, wh, preferred_element_type=F32)
           + jnp.dot(xh, wl, preferred_element_type=F32))
    beta = jax.nn.sigmoid(acc)
    s = acc + dtb_ref[...]
    softplus = jnp.maximum(s, 0.0) + jnp.log1p(jnp.exp(-jnp.abs(s)))
    g = nega_ref[...] * softplus
    o_ref[...] = jnp.where(isa_ref[...] > 0.5, g, beta)


def _gates(x2, w_gate, is_a, neg_a, dtb, tm):
    t, d = x2.shape
    tm = min(tm, t)
    vec = pl.BlockSpec((1, GATE_W), lambda i: (0, 0))
    return pl.pallas_call(
        _gate_kernel,
        grid=(t // tm,),
        in_specs=[pl.BlockSpec((tm, d), lambda i: (i, 0)),
                  pl.BlockSpec((d, GATE_W), lambda i: (0, 0)), vec, vec, vec],
        out_specs=pl.BlockSpec((tm, GATE_W), lambda i: (i, 0)),
        out_shape=jax.ShapeDtypeStruct((t, GATE_W), F32),
        compiler_params=_cparams(("parallel",)),
        name="gdn_gates",
    )(x2, w_gate, is_a, neg_a, dtb)


def _each(fn, *lists):
    return [fn(*args) for args in zip(*lists)]


def _unit_lower_inverse_minus_identity(lows, same16, same32):
    c = lows[0].shape[0]
    d = _each(lambda low: jnp.where(same16, low, 0.0), lows)
    x = _each(lambda a: -a, d)
    p = _each(lambda a: _bdot(a, a), d)
    xp = _each(lambda a, b: _bdot(jnp.concatenate([a, b], axis=0), b), x, p)
    n = _each(lambda a, b, ab: a + b + ab[:c], x, p, xp)
    p2 = _each(lambda ab: ab[c:], xp)
    np2 = _each(lambda a, b: _bdot(jnp.concatenate([a, b], axis=0), b), n, p2)
    n = _each(lambda a, b, ab: a + b + ab[:c], n, p2, np2)
    p4 = _each(lambda ab: ab[c:], np2)
    n = _each(lambda a, b: a + b + _bdot(a, b), n, p4)
    only32 = jnp.logical_and(same32, jnp.logical_not(same16))
    off1 = _each(lambda low: jnp.where(only32, low, 0.0), lows)
    c1 = _each(lambda a, o: o + _bdot(a, o), n, off1)
    n = _each(lambda a, cc: a - (cc + _bdot(cc, a)), n, c1)
    off2 = _each(lambda low: jnp.where(same32, 0.0, low), lows)
    c2 = _each(lambda a, o: o + _bdot(a, o), n, off2)
    n = _each(lambda a, cc: a - (cc + _bdot(cc, a)), n, c2)
    return n


def _gdn_kernel(pq_ref, pk_ref, pv_ref, pz_ref, cwq_ref, cwk_ref, cwv_ref, gcol_ref, grow_ref,
                nw_ref, o_ref, hq_ref, hk_ref, hv_ref, state_ref, *, hb):
    c_idx = pl.program_id(2)
    C = CHUNK
    H0 = V7X_SUBLANES
    w_blk = hb * HEAD_W

    @pl.when(c_idx == 0)
    def _():
        state_ref[...] = jnp.zeros_like(state_ref)
        zero = jnp.zeros((H0, w_blk), F32)
        hq_ref[0:H0, :] = zero
        hk_ref[0:H0, :] = zero
        hv_ref[0:H0, :] = zero

    def conv_silu(p_ref, h_ref, cw_ref):
        h_ref[H0:H0 + C, :] = p_ref[...].astype(F32)
        e = h_ref[...]
        w = cw_ref[...]
        t = e * w[0:1, :]
        for j in range(1, CONV_TAPS):
            t = pltpu.roll(t, 1, axis=0) + e * w[j:j + 1, :]
        h_ref[0:H0, :] = e[C:C + H0, :]
        y = t[H0:H0 + C, :]
        return y * jax.nn.sigmoid(y)

    yq = conv_silu(pq_ref, hq_ref, cwq_ref)
    yk = conv_silu(pk_ref, hk_ref, cwk_ref)
    yv = conv_silu(pv_ref, hv_ref, cwv_ref)
    z = pz_ref[...].astype(F32)

    row = lax.broadcasted_iota(jnp.int32, (C, C), 0)
    col = lax.broadcasted_iota(jnp.int32, (C, C), 1)
    causal = row >= col
    strict = row > col
    same16 = (row // 16) == (col // 16)
    same32 = (row // 32) == (col // 32)
    tril = jnp.where(causal, 1.0, 0.0).astype(BF16)
    triu = jnp.where(row <= col, 1.0, 0.0).astype(BF16)

    gcol = gcol_ref[...]
    grow = grow_ref[...]
    gh, gm, gl = _split3(gcol[:, hb:])
    gc_col = (jnp.dot(tril, gh, preferred_element_type=F32)
              + jnp.dot(tril, gm, preferred_element_type=F32)
              + jnp.dot(tril, gl, preferred_element_type=F32))
    rh, rm, rl = _split3(grow[hb:, :])
    gc_row = (jnp.dot(rh, triu, preferred_element_type=F32)
              + jnp.dot(rm, triu, preferred_element_type=F32)
              + jnp.dot(rl, triu, preferred_element_type=F32))

    nw = nw_ref[...]
    q_scale = HEAD_W ** -0.5

    heads = list(range(hb))
    sl = [slice(h * HEAD_W, (h + 1) * HEAD_W) for h in heads]

    def l2n(y, scale):
        return y * (lax.rsqrt(jnp.sum(y * y, axis=-1, keepdims=True) + GDN_EPS) * scale)

    q = [l2n(yq[:, s], q_scale) for s in sl]
    k = [l2n(yk[:, s], 1.0) for s in sl]
    v = [yv[:, s] for s in sl]
    beta = [gcol[:, h:h + 1] for h in heads]
    g_c = [gc_col[:, h:h + 1] for h in heads]
    g_r = [gc_row[h:h + 1, :] for h in heads]
    g_last = [r[:, C - 1:C] for r in g_r]
    decay = _each(lambda gc, gr: jnp.exp(jnp.where(causal, gc - gr, -jnp.inf)), g_c, g_r)
    eg = _each(jnp.exp, g_c)
    k_beta = _each(lambda a, b: a * b, k, beta)
    kq = _each(lambda kb, qq, kk: _bdot_nt(jnp.concatenate([kb, qq], axis=0), kk), k_beta, q, k)
    low = _each(lambda a, dc: jnp.where(strict, a[:C] * dc, 0.0), kq, decay)
    attn = _each(lambda a, dc: a[C:] * dc, kq, decay)
    n_inv = _unit_lower_inverse_minus_identity(low, same16, same32)
    rhs = _each(lambda vv, b, kb, e: jnp.concatenate([vv * b, kb * e], axis=1), v, beta, k_beta, eg)
    sol = _each(lambda r, n: r + _bdot(n, r), rhs, n_inv)
    state = [state_ref[h] for h in heads]
    ws = _each(lambda s, qq, e, st: _bdot(jnp.concatenate([s[:, HEAD_W:], qq * e], axis=0), st),
               sol, q, eg, state)
    v_new = _each(lambda s, a: s[:, :HEAD_W] - a[:C], sol, ws)
    o = _each(lambda a, at, vn: a[C:] + _bdot(at, vn), ws, attn, v_new)
    k_dec = _each(lambda kk, gl_, gc: kk * jnp.exp(gl_ - gc), k, g_last, g_c)
    new_state = _each(lambda st, gl_, kd, vn: st * jnp.exp(gl_) + _bdot_tn(kd, vn),
                      state, g_last, k_dec, v_new)
    for h in heads:
        state_ref[h] = new_state[h]
    o = _each(lambda a: a * lax.rsqrt(jnp.mean(a * a, axis=-1, keepdims=True) + GDN_EPS) * nw, o)
    for h in heads:
        zh = z[:, sl[h]]
        o_ref[:, sl[h]] = (o[h] * (zh * jax.nn.sigmoid(zh))).astype(o_ref.dtype)


def _gdn_core(p3, conv_w, gcol, grow, norm_w, hb):
    b, s, w4 = p3.shape
    v_w = w4 // 4
    heads = v_w // HEAD_W
    groups = heads // hb
    wb = hb * HEAD_W
    nch = s // CHUNK

    def pspec(part):
        return pl.BlockSpec((None, CHUNK, wb), lambda bi, gi, ci, part=part: (bi, ci, part * groups + gi))

    def cspec(part):
        return pl.BlockSpec((CONV_TAPS, wb), lambda bi, gi, ci, part=part: (0, part * groups + gi))

    return pl.pallas_call(
        functools.partial(_gdn_kernel, hb=hb),
        grid=(b, groups, nch),
        in_specs=[pspec(0), pspec(1), pspec(2), pspec(3), cspec(0), cspec(1), cspec(2),
                  pl.BlockSpec((None, None, CHUNK, 2 * hb), lambda bi, gi, ci: (bi, gi, ci, 0)),
                  pl.BlockSpec((None, None, None, 2 * hb, CHUNK), lambda bi, gi, ci: (bi, gi, ci, 0, 0)),
                  pl.BlockSpec((1, HEAD_W), lambda bi, gi, ci: (0, 0))],
        out_specs=pl.BlockSpec((None, CHUNK, wb), lambda bi, gi, ci: (bi, ci, gi)),
        out_shape=jax.ShapeDtypeStruct((b, s, v_w), BF16),
        scratch_shapes=[pltpu.VMEM((CHUNK + V7X_SUBLANES, wb), F32),
                        pltpu.VMEM((CHUNK + V7X_SUBLANES, wb), F32),
                        pltpu.VMEM((CHUNK + V7X_SUBLANES, wb), F32),
                        pltpu.VMEM((hb, HEAD_W, HEAD_W), F32)],
        compiler_params=_cparams(("parallel", "parallel", "arbitrary")),
        name="gdn_core",
    )(p3, p3, p3, p3, conv_w, conv_w, conv_w, gcol, grow, norm_w)


def _layer_norm_rows(y, g, b):
    mu = jnp.mean(y, axis=-1, keepdims=True)
    yc = y - mu
    var = jnp.mean(yc * yc, axis=-1, keepdims=True)
    return yc * lax.rsqrt(var + LN_EPS) * g + b


def _proj_ln_kernel(a_ref, w_ref, x_ref, g_ref, b_ref, o_ref, *, alpha):
    h = jnp.dot(a_ref[...].astype(BF16), w_ref[...], preferred_element_type=F32)
    o_ref[...] = _layer_norm_rows(alpha * x_ref[...] + h, g_ref[...], b_ref[...])


def _proj_ln(a, w, x2, g, b, alpha, tm):
    t, k = a.shape
    d = w.shape[1]
    tm = min(tm, t)
    vec = pl.BlockSpec((1, d), lambda i: (0, 0))
    return pl.pallas_call(
        functools.partial(_proj_ln_kernel, alpha=alpha),
        grid=(t // tm,),
        in_specs=[pl.BlockSpec((tm, k), lambda i: (i, 0)),
                  pl.BlockSpec((k, d), lambda i: (0, 0)),
                  pl.BlockSpec((tm, d), lambda i: (i, 0)), vec, vec],
        out_specs=pl.BlockSpec((tm, d), lambda i: (i, 0)),
        out_shape=jax.ShapeDtypeStruct((t, d), F32),
        compiler_params=_cparams(("parallel",)),
        name="proj_ln",
    )(a, w, x2, g, b)


def _mlp_ln_kernel(x_ref, wu_ref, wd_ref, g_ref, b_ref, o_ref, xb_ref, acc_ref, *, alpha):
    f = pl.program_id(1)

    @pl.when(f == 0)
    def _():
        xb_ref[...] = x_ref[...].astype(BF16)
        acc_ref[...] = jnp.zeros_like(acc_ref)

    h = jnp.dot(xb_ref[...], wu_ref[...], preferred_element_type=F32)
    h = jnp.maximum(h, 0.0)
    h = h * h
    acc_ref[...] += jnp.dot(h.astype(BF16), wd_ref[...], preferred_element_type=F32)

    @pl.when(f == pl.num_programs(1) - 1)
    def _():
        o_ref[...] = _layer_norm_rows(alpha * x_ref[...] + acc_ref[...], g_ref[...], b_ref[...])


def _mlp_ln(x2, w_up, w_down, g, b, alpha, tm, tf):
    t, d = x2.shape
    ff = w_up.shape[1]
    tm = min(tm, t)
    tf = min(tf, ff)
    vec = pl.BlockSpec((1, d), lambda i, f: (0, 0))
    return pl.pallas_call(
        functools.partial(_mlp_ln_kernel, alpha=alpha),
        grid=(t // tm, ff // tf),
        in_specs=[pl.BlockSpec((tm, d), lambda i, f: (i, 0)),
                  pl.BlockSpec((d, tf), lambda i, f: (0, f)),
                  pl.BlockSpec((tf, d), lambda i, f: (f, 0)), vec, vec],
        out_specs=pl.BlockSpec((tm, d), lambda i, f: (i, 0)),
        out_shape=jax.ShapeDtypeStruct((t, d), F32),
        scratch_shapes=[pltpu.VMEM((tm, d), BF16), pltpu.VMEM((tm, d), F32)],
        compiler_params=_cparams(("parallel", "arbitrary")),
        name="mlp_ln",
    )(x2, w_up, w_down, g, b)


def _diff_attn_kernel(q_ref, k_ref, v_ref, lam_ref, sw_ref, o_ref, qs_ref, s_ref, acc_ref,
                      *, tq, tk, lambda_init):
    i = pl.program_id(2)
    half = HEAD_W // 2
    lane = lax.broadcasted_iota(jnp.int32, (tq, HEAD_W), 1)
    q = q_ref[...]
    zero = jnp.zeros_like(q)
    qs_ref[0:tq, :] = jnp.where(lane < half, q, zero)
    qs_ref[tq:2 * tq, :] = jnp.where(lane >= half, q, zero)
    acc_ref[...] = jnp.zeros_like(acc_ref)
    cw = min(V7X_MXU_COLS, tq)
    assert tq % cw == 0
    ones_rows = jnp.ones((ONES_ROWS, tk), BF16)

    def produce(slot, j):
        kb = k_ref[pl.ds(pl.multiple_of(j * tk, tk), tk), :]
        s_ref[slot] = lax.dot_general(kb, qs_ref[...], (((1,), (1,)), ((), ())), preferred_element_type=F32)

    def consume(slot, j, m_prev, diag):
        vb = v_ref[pl.ds(pl.multiple_of(j * tk, tk), tk), :]
        lhs = jnp.concatenate([vb.T, ones_rows], axis=0)
        m_parts = []
        for c in range(2 * tq // cw):
            cs = slice(c * cw, (c + 1) * cw)
            r_min = (c * cw) % tq
            m_prev_c = m_prev[:, cs]
            if diag is not None and diag * tk > r_min + cw - 1:
                m_parts.append(m_prev_c)
                continue
            sc = s_ref[slot, :, cs]
            if diag is not None and diag * tk + tk - 1 > r_min:
                key = lax.broadcasted_iota(jnp.int32, (tk, cw), 0) + diag * tk
                qry = lax.broadcasted_iota(jnp.int32, (tk, cw), 1) + r_min
                sc = jnp.where(key <= qry, sc, -jnp.inf)
            m_new_c = jnp.maximum(m_prev_c, jnp.max(sc, axis=0, keepdims=True))
            alpha_c = jnp.exp2(m_prev_c - m_new_c)
            p_c = jnp.exp2(sc - m_new_c).astype(BF16)
            pv_c = jnp.dot(lhs, p_c, preferred_element_type=F32)
            acc_ref[:, cs] = alpha_c * acc_ref[:, cs] + pv_c
            m_parts.append(m_new_c)
        return jnp.concatenate(m_parts, axis=1)

    assert tq == 2 * tk

    def pair(jj, m):
        produce(1, 2 * jj + 1)
        m = consume(0, 2 * jj, m, None)
        produce(0, 2 * jj + 2)
        return consume(1, 2 * jj + 1, m, None)

    produce(0, 0)
    m = lax.fori_loop(0, i, pair, jnp.full((1, 2 * tq), -jnp.inf, F32))
    produce(1, 2 * i + 1)
    m = consume(0, 2 * i, m, 0)
    m = consume(1, 2 * i + 1, m, 1)

    lp = lam_ref[...]
    lam = (jnp.exp(jnp.sum(lp[0:1, :] * lp[1:2, :], axis=-1, keepdims=True))
           - jnp.exp(jnp.sum(lp[2:3, :] * lp[3:4, :], axis=-1, keepdims=True)) + lambda_init)
    o_all = acc_ref[0:HEAD_W, :] * (1.0 / acc_ref[HEAD_W:HEAD_W + 1, :])
    o = (o_all[:, 0:tq] - lam * o_all[:, tq:2 * tq]).T
    o = o * lax.rsqrt(jnp.mean(o * o, axis=-1, keepdims=True) + SUBLN_EPS) * sw_ref[...]
    o_ref[...] = (o * (1.0 - lambda_init)).astype(o_ref.dtype)


def _diff_attention(q3, kv3, lam_params, subln_w, lambda_init, tq, tk):
    b, s, wq = q3.shape
    heads = wq // HEAD_W
    tq = min(tq, s)
    tk = min(tk, tq)
    return pl.pallas_call(
        functools.partial(_diff_attn_kernel, tq=tq, tk=tk, lambda_init=lambda_init),
        grid=(b, heads, s // tq),
        in_specs=[pl.BlockSpec((None, tq, HEAD_W), lambda bi, h, i: (bi, i, h)),
                  pl.BlockSpec((None, s, HEAD_W), lambda bi, h, i: (bi, 0, h)),
                  pl.BlockSpec((None, s, HEAD_W), lambda bi, h, i, heads=heads: (bi, 0, heads + h)),
                  pl.BlockSpec(lam_params.shape, lambda bi, h, i: (0, 0)),
                  pl.BlockSpec((1, HEAD_W), lambda bi, h, i: (0, 0))],
        out_specs=pl.BlockSpec((None, tq, HEAD_W), lambda bi, h, i: (bi, i, h)),
        out_shape=jax.ShapeDtypeStruct((b, s, wq), BF16),
        scratch_shapes=[pltpu.VMEM((2 * tq, HEAD_W), BF16),
                        pltpu.VMEM((2, tk, 2 * tq), F32),
                        pltpu.VMEM((HEAD_W + ONES_ROWS, 2 * tq), F32)],
        compiler_params=_cparams(("parallel", "parallel", "arbitrary")),
        name="diff_attn",
    )(q3, kv3, kv3, lam_params, subln_w)


def _gdn_layer(x2, bsz, seq, w_in, conv_w, a_log, dt_bias, norm_w, w_out, ln_g, ln_b, alpha, hb, tiles):
    t, d = x2.shape
    heads = a_log.shape[0]
    v_w = heads * HEAD_W
    groups = heads // hb
    main = 4 * v_w
    p = _matmul(x2, w_in[:, :main].astype(BF16), BF16, tiles["tm"], tiles["tn"], name="gdn_in_proj")
    wb = w_in[:, main:main + heads].reshape(d, groups, hb)
    wa = w_in[:, main + heads:main + 2 * heads].reshape(d, groups, hb)
    w_gate = jnp.concatenate([wb, wa], axis=2).reshape(d, 2 * heads)
    w_gate = jnp.pad(w_gate, ((0, 0), (0, GATE_W - 2 * heads)))

    def per_col(vals, fill):
        zeros = jnp.full((groups, hb), fill, F32)
        cols = jnp.concatenate([zeros, vals.astype(F32).reshape(groups, hb)], axis=1).reshape(1, 2 * heads)
        return jnp.pad(cols, ((0, 0), (0, GATE_W - 2 * heads)), constant_values=fill)

    is_a = per_col(jnp.ones((heads,), F32), 0.0)
    neg_a = per_col(-jnp.exp(a_log.astype(F32)), 0.0)
    dtb = per_col(dt_bias, 0.0)
    gates = _gates(x2, w_gate, is_a, neg_a, dtb, tiles["tm_gate"])[:, :2 * heads]
    gcol = gates.reshape(bsz, seq, groups, 2 * hb).transpose(0, 2, 1, 3)
    grow = gates.reshape(bsz, seq // CHUNK, CHUNK, groups, 2 * hb).transpose(0, 3, 1, 4, 2)
    o = _gdn_core(p.reshape(bsz, seq, main), conv_w, gcol, grow, norm_w.reshape(1, HEAD_W), hb)
    return _proj_ln(o.reshape(t, v_w), w_out.astype(BF16), x2, ln_g.reshape(1, d), ln_b.reshape(1, d),
                    alpha, tiles["tm_proj"])


def _tiles():
    return dict(tm=1024, tn=1024, tm_gate=512, tm_proj=512, tm_mlp=512, tf_mlp=1024, tq=512, tk=256, hb=16)


def _forward(x, gdn_w_in, gdn_conv_w, gdn_a_log, gdn_dt_bias, gdn_norm_w, gdn_w_out, diff_w_q, diff_lambda,
             diff_subln_w, diff_w_o, shared_w_kv, mlp_w_up, mlp_w_down, ln_g, ln_b, tiles):
    bsz, seq, d = x.shape
    t = bsz * seq
    depth = mlp_w_up.shape[0]
    n_a = gdn_w_in.shape[0]
    alpha = (2 * depth) ** 0.25
    x2 = x.reshape(t, d).astype(F32)
    kv3 = None
    for l in range(depth):
        if l < n_a:
            x2 = _gdn_layer(x2, bsz, seq, gdn_w_in[l], gdn_conv_w[l], gdn_a_log[l], gdn_dt_bias[l],
                            gdn_norm_w[l], gdn_w_out[l], ln_g[l, 0], ln_b[l, 0], alpha,
                            min(tiles["hb"], gdn_a_log.shape[1]), tiles)
        else:
            j = l - n_a
            lambda_init = 0.8 - 0.6 * math.exp(-0.3 * l)
            dk = diff_lambda.shape[-1]
            q = _matmul(x2, diff_w_q[j].astype(BF16), BF16, tiles["tm"], tiles["tn"],
                        scale=dk ** -0.5 * math.log2(math.e), name="diff_q_proj")
            o = _diff_attention(q.reshape(bsz, seq, -1), kv3, diff_lambda[j].astype(F32),
                                diff_subln_w[j].reshape(1, HEAD_W).astype(F32), lambda_init, tiles["tq"],
                                tiles["tk"])
            x2 = _proj_ln(o.reshape(t, -1), diff_w_o[j].astype(BF16), x2, ln_g[l, 0].reshape(1, d),
                          ln_b[l, 0].reshape(1, d), alpha, tiles["tm_proj"])
        x2 = _mlp_ln(x2, mlp_w_up[l].astype(BF16), mlp_w_down[l].astype(BF16), ln_g[l, 1].reshape(1, d),
                     ln_b[l, 1].reshape(1, d), alpha, tiles["tm_mlp"], tiles["tf_mlp"])
        if l == n_a - 1:
            kv = _matmul(x2, shared_w_kv.astype(BF16), BF16, tiles["tm"], tiles["tn"], name="shared_kv_proj")
            kv3 = kv.reshape(bsz, seq, -1)
    return x2.reshape(bsz, seq, d).astype(x.dtype)


def kernel(x, gdn_w_in, gdn_conv_w, gdn_a_log, gdn_dt_bias, gdn_norm_w, gdn_w_out, diff_w_q, diff_lambda,
           diff_subln_w, diff_w_o, shared_w_kv, mlp_w_up, mlp_w_down, ln_g, ln_b):
    return _forward(x, gdn_w_in, gdn_conv_w, gdn_a_log, gdn_dt_bias, gdn_norm_w, gdn_w_out, diff_w_q,
                    diff_lambda, diff_subln_w, diff_w_o, shared_w_kv, mlp_w_up, mlp_w_down, ln_g, ln_b,
                    _tiles())
```

```python
import functools
import math

import jax
import jax.numpy as jnp
from jax import lax
from jax.experimental import pallas as pl
from jax.experimental.pallas import tpu as pltpu

F32 = jnp.float32
BF16 = jnp.bfloat16

HEAD_W = 128
CHUNK = 64
CONV_TAPS = 4
GDN_EPS = 1e-6
SUBLN_EPS = 1e-5
LN_EPS = 1e-5
GATE_W = 128
LN_ROW_CHUNKS = 4

V7X_SUBLANES = 8
V7X_MXU_COLS = 256
ONES_ROWS = 16
V7X_VMEM_LIMIT_BYTES = 56 * 1024 * 1024


def _cparams(sem):
    return pltpu.CompilerParams(dimension_semantics=sem, vmem_limit_bytes=V7X_VMEM_LIMIT_BYTES)


def _bdot(a, b):
    return jnp.dot(a.astype(BF16), b.astype(BF16), preferred_element_type=F32)


def _bdot_nt(a, b):
    return lax.dot_general(a.astype(BF16), b.astype(BF16), (((1,), (1,)), ((), ())),
                           preferred_element_type=F32)


def _bdot_tn(a, b):
    return lax.dot_general(a.astype(BF16), b.astype(BF16), (((0,), (0,)), ((), ())),
                           preferred_element_type=F32)


def _split3(x):
    h = x.astype(BF16)
    r = x - h.astype(F32)
    m = r.astype(BF16)
    l = (r - m.astype(F32)).astype(BF16)
    return h, m, l


def _mm_kernel(x_ref, w_ref, o_ref, *, scale):
    acc = jnp.dot(x_ref[...].astype(BF16), w_ref[...], preferred_element_type=F32)
    if scale != 1.0:
        acc = acc * scale
    o_ref[...] = acc.astype(o_ref.dtype)


def _matmul(x, w, out_dtype, tm, tn, scale=1.0, name="matmul"):
    m, k = x.shape
    n = w.shape[1]
    tm = min(tm, m)
    tn = min(tn, n)
    return pl.pallas_call(
        functools.partial(_mm_kernel, scale=scale),
        grid=(m // tm, n // tn),
        in_specs=[pl.BlockSpec((tm, k), lambda i, j: (i, 0)),
                  pl.BlockSpec((k, tn), lambda i, j: (0, j))],
        out_specs=pl.BlockSpec((tm, tn), lambda i, j: (i, j)),
        out_shape=jax.ShapeDtypeStruct((m, n), out_dtype),
        compiler_params=_cparams(("parallel", "parallel")),
        name=name,
    )(x, w)


def _gate_kernel(x_ref, w_ref, isa_ref, nega_ref, dtb_ref, o_ref):
    x = x_ref[...]
    w = w_ref[...]
    xh = x.astype(BF16)
    xl = (x - xh.astype(F32)).astype(BF16)
    wh = w.astype(BF16)
    wl = (w - wh.astype(F32)).astype(BF16)
    acc = (jnp.dot(xh, wh, preferred_element_type=F32)
           + jnp.dot(xl, wh, preferred_element_type=F32)
           + jnp.dot(xh, wl, preferred_element_type=F32))
    beta = jax.nn.sigmoid(acc)
    s = acc + dtb_ref[...]
    softplus = jnp.maximum(s, 0.0) + jnp.log1p(jnp.exp(-jnp.abs(s)))
    g = nega_ref[...] * softplus
    o_ref[...] = jnp.where(isa_ref[...] > 0.5, g, beta)


def _gates(x2, w_gate, is_a, neg_a, dtb, tm):
    t, d = x2.shape
    tm = min(tm, t)
    vec = pl.BlockSpec((1, GATE_W), lambda i: (0, 0))
    return pl.pallas_call(
        _gate_kernel,
        grid=(t // tm,),
        in_specs=[pl.BlockSpec((tm, d), lambda i: (i, 0)),
                  pl.BlockSpec((d, GATE_W), lambda i: (0, 0)), vec, vec, vec],
        out_specs=pl.BlockSpec((tm, GATE_W), lambda i: (i, 0)),
        out_shape=jax.ShapeDtypeStruct((t, GATE_W), F32),
        compiler_params=_cparams(("parallel",)),
        name="gdn_gates",
    )(x2, w_gate, is_a, neg_a, dtb)


def _each(fn, *lists):
    return [fn(*args) for args in zip(*lists)]


def _unit_lower_inverse_minus_identity(lows, same16, same32):
    c = lows[0].shape[0]
    d = _each(lambda low: jnp.where(same16, low, 0.0), lows)
    x = _each(lambda a: -a, d)
    p = _each(lambda a: _bdot(a, a), d)
    xp = _each(lambda a, b: _bdot(jnp.concatenate([a, b], axis=0), b), x, p)
    n = _each(lambda a, b, ab: a + b + ab[:c], x, p, xp)
    p2 = _each(lambda ab: ab[c:], xp)
    np2 = _each(lambda a, b: _bdot(jnp.concatenate([a, b], axis=0), b), n, p2)
    n = _each(lambda a, b, ab: a + b + ab[:c], n, p2, np2)
    p4 = _each(lambda ab: ab[c:], np2)
    n = _each(lambda a, b: a + b + _bdot(a, b), n, p4)
    only32 = jnp.logical_and(same32, jnp.logical_not(same16))
    off1 = _each(lambda low: jnp.where(only32, low, 0.0), lows)
    c1 = _each(lambda a, o: o + _bdot(a, o), n, off1)
    n = _each(lambda a, cc: a - (cc + _bdot(cc, a)), n, c1)
    off2 = _each(lambda low: jnp.where(same32, 0.0, low), lows)
    c2 = _each(lambda a, o: o + _bdot(a, o), n, off2)
    n = _each(lambda a, cc: a - (cc + _bdot(cc, a)), n, c2)
    return n


def _gdn_kernel(pq_ref, pk_ref, pv_ref, pz_ref, cwq_ref, cwk_ref, cwv_ref, gcol_ref, grow_ref,
                nw_ref, o_ref, hq_ref, hk_ref, hv_ref, state_ref, *, hb, nc):
    c_idx = pl.program_id(2)
    C = CHUNK
    w_blk = hb * HEAD_W
    chunks = list(range(nc))
    rows = [slice(cc * C, (cc + 1) * C) for cc in chunks]

    @pl.when(c_idx == 0)
    def _():
        state_ref[...] = jnp.zeros_like(state_ref)
        zero = jnp.zeros((C, w_blk), BF16)
        hq_ref[0:C, :] = zero
        hk_ref[0:C, :] = zero
        hv_ref[0:C, :] = zero

    sr = lax.broadcasted_iota(jnp.int32, ((CONV_TAPS - 1) * C, 2 * C), 0)
    sc = lax.broadcasted_iota(jnp.int32, ((CONV_TAPS - 1) * C, 2 * C), 1)
    tap = sr // C
    shift_mat = jnp.where(sc == sr - tap * (C - 1) + (C - (CONV_TAPS - 1)), 1.0, 0.0).astype(BF16)

    def conv_silu(p_ref, h_ref, cw_ref):
        h_ref[C:(nc + 1) * C, :] = p_ref[...]
        w = cw_ref[...]
        out = []
        for cc in chunks:
            window = h_ref[cc * C:(cc + 2) * C, :]
            shifted = jnp.dot(shift_mat, window, preferred_element_type=F32)
            y = shifted[0:C] * w[0:1, :]
            for j in range(1, CONV_TAPS - 1):
                y = y + shifted[j * C:(j + 1) * C] * w[j:j + 1, :]
            y = y + window[C:2 * C].astype(F32) * w[CONV_TAPS - 1:CONV_TAPS, :]
            out.append(y * jax.nn.sigmoid(y))
        h_ref[0:C, :] = h_ref[nc * C:(nc + 1) * C, :]
        return out

    yq = conv_silu(pq_ref, hq_ref, cwq_ref)
    yk = conv_silu(pk_ref, hk_ref, cwk_ref)
    yv = conv_silu(pv_ref, hv_ref, cwv_ref)

    row = lax.broadcasted_iota(jnp.int32, (C, C), 0)
    col = lax.broadcasted_iota(jnp.int32, (C, C), 1)
    causal = row >= col
    strict = row > col
    same16 = (row // 16) == (col // 16)
    same32 = (row // 32) == (col // 32)
    tril = jnp.where(causal, 1.0, 0.0).astype(BF16)
    triu = jnp.where(row <= col, 1.0, 0.0).astype(BF16)

    gcol = gcol_ref[...]
    gc_cols, gc_rows, grows = [], [], []
    for cc in chunks:
        gh, gm, gl = _split3(gcol[rows[cc], hb:])
        gc_cols.append(jnp.dot(tril, gh, preferred_element_type=F32)
                       + jnp.dot(tril, gm, preferred_element_type=F32)
                       + jnp.dot(tril, gl, preferred_element_type=F32))
        grow = grow_ref[cc]
        rh, rm, rl = _split3(grow[hb:, :])
        gc_rows.append(jnp.dot(rh, triu, preferred_element_type=F32)
                       + jnp.dot(rm, triu, preferred_element_type=F32)
                       + jnp.dot(rl, triu, preferred_element_type=F32))

    nw = nw_ref[...]
    q_scale = HEAD_W ** -0.5

    heads = list(range(hb))
    sl = [slice(h * HEAD_W, (h + 1) * HEAD_W) for h in heads]
    items = [(cc, h) for cc in chunks for h in heads]

    def l2n(y, scale):
        return y * (lax.rsqrt(jnp.sum(y * y, axis=-1, keepdims=True) + GDN_EPS) * scale)

    q = [l2n(yq[cc][:, sl[h]], q_scale) for cc, h in items]
    k = [l2n(yk[cc][:, sl[h]], 1.0) for cc, h in items]
    v = [yv[cc][:, sl[h]] for cc, h in items]
    beta = [gcol[rows[cc], h:h + 1] for cc, h in items]
    g_c = [gc_cols[cc][:, h:h + 1] for cc, h in items]
    g_r = [gc_rows[cc][h:h + 1, :] for cc, h in items]
    g_last = [r[:, C - 1:C] for r in g_r]
    decay = _each(lambda gc, gr: jnp.exp(jnp.where(causal, gc - gr, -jnp.inf)), g_c, g_r)
    eg = _each(jnp.exp, g_c)
    k_beta = _each(lambda a, b: a * b, k, beta)
    kq = _each(lambda kb, qq, kk: _bdot_nt(jnp.concatenate([kb, qq], axis=0), kk), k_beta, q, k)
    low = _each(lambda a, dc: jnp.where(strict, a[:C] * dc, 0.0), kq, decay)
    attn = _each(lambda a, dc: a[C:] * dc, kq, decay)
    n_inv = _unit_lower_inverse_minus_identity(low, same16, same32)
    rhs = _each(lambda vv, b, kb, e: jnp.concatenate([vv * b, kb * e], axis=1), v, beta, k_beta, eg)
    sol = _each(lambda r, n: r + _bdot(n, r), rhs, n_inv)
    wq = _each(lambda s, qq, e: jnp.concatenate([s[:, HEAD_W:], qq * e], axis=0), sol, q, eg)
    k_dec = _each(lambda kk, gl_, gc: kk * jnp.exp(gl_ - gc), k, g_last, g_c)
    e_last = _each(jnp.exp, g_last)

    state = [state_ref[h] for h in heads]
    o = []
    for cc in chunks:
        it = slice(cc * hb, (cc + 1) * hb)
        ws = _each(_bdot, wq[it], state)
        v_new = _each(lambda s, a: s[:, :HEAD_W] - a[:C], sol[it], ws)
        o += _each(lambda a, at, vn: a[C:] + _bdot(at, vn), ws, attn[it], v_new)
        state = _each(lambda st, el, kd, vn: st * el + _bdot_tn(kd, vn), state, e_last[it], k_dec[it], v_new)
    for h in heads:
        state_ref[h] = state[h]
    o = _each(lambda a: a * lax.rsqrt(jnp.mean(a * a, axis=-1, keepdims=True) + GDN_EPS) * nw, o)
    for idx, (cc, h) in enumerate(items):
        zh = pz_ref[rows[cc], sl[h]].astype(F32)
        o_ref[rows[cc], sl[h]] = (o[idx] * (zh * jax.nn.sigmoid(zh))).astype(o_ref.dtype)


def _gdn_core(p3, conv_w, gcol, grow, norm_w, hb, nc):
    b, s, w4 = p3.shape
    assert p3.dtype == BF16
    v_w = w4 // 4
    heads = v_w // HEAD_W
    groups = heads // hb
    wb = hb * HEAD_W
    tc = nc * CHUNK
    assert s % tc == 0

    def pspec(part):
        return pl.BlockSpec((None, tc, wb), lambda bi, gi, ci, part=part: (bi, ci, part * groups + gi))

    def cspec(part):
        return pl.BlockSpec((CONV_TAPS, wb), lambda bi, gi, ci, part=part: (0, part * groups + gi))

    halo = pltpu.VMEM(((nc + 1) * CHUNK, wb), BF16)
    return pl.pallas_call(
        functools.partial(_gdn_kernel, hb=hb, nc=nc),
        grid=(b, groups, s // tc),
        in_specs=[pspec(0), pspec(1), pspec(2), pspec(3), cspec(0), cspec(1), cspec(2),
                  pl.BlockSpec((None, None, tc, 2 * hb), lambda bi, gi, ci: (bi, gi, ci, 0)),
                  pl.BlockSpec((None, None, nc, 2 * hb, CHUNK), lambda bi, gi, ci: (bi, gi, ci, 0, 0)),
                  pl.BlockSpec((1, HEAD_W), lambda bi, gi, ci: (0, 0))],
        out_specs=pl.BlockSpec((None, tc, wb), lambda bi, gi, ci: (bi, ci, gi)),
        out_shape=jax.ShapeDtypeStruct((b, s, v_w), BF16),
        scratch_shapes=[halo, halo, halo, pltpu.VMEM((hb, HEAD_W, HEAD_W), F32)],
        compiler_params=_cparams(("parallel", "parallel", "arbitrary")),
        name="gdn_core",
    )(p3, p3, p3, p3, conv_w, conv_w, conv_w, gcol, grow, norm_w)


def _layer_norm_rows(y, g, b):
    mu = jnp.mean(y, axis=-1, keepdims=True)
    yc = y - mu
    var = jnp.mean(yc * yc, axis=-1, keepdims=True)
    return yc * lax.rsqrt(var + LN_EPS) * g + b


def _row_chunks(rows):
    n = LN_ROW_CHUNKS if rows % (LN_ROW_CHUNKS * V7X_SUBLANES) == 0 else 1
    step = rows // n
    return [slice(k * step, (k + 1) * step) for k in range(n)]


def _matmul_then_ln(chunks, matmul_rows, ln_store_rows):
    pending = None
    for rs in chunks:
        h = matmul_rows(rs)
        if pending is not None:
            ln_store_rows(*pending)
        pending = (rs, h)
    ln_store_rows(*pending)


def _proj_ln_kernel(a_ref, w_ref, x_ref, g_ref, b_ref, o_ref, *, alpha):
    def matmul_rows(rs):
        return jnp.dot(a_ref[rs, :].astype(BF16), w_ref[...], preferred_element_type=F32)

    def ln_store_rows(rs, h):
        o_ref[rs, :] = _layer_norm_rows(alpha * x_ref[rs, :] + h, g_ref[...], b_ref[...])

    _matmul_then_ln(_row_chunks(a_ref.shape[0]), matmul_rows, ln_store_rows)


def _proj_ln(a, w, x2, g, b, alpha, tm):
    t, k = a.shape
    d = w.shape[1]
    tm = min(tm, t)
    vec = pl.BlockSpec((1, d), lambda i: (0, 0))
    return pl.pallas_call(
        functools.partial(_proj_ln_kernel, alpha=alpha),
        grid=(t // tm,),
        in_specs=[pl.BlockSpec((tm, k), lambda i: (i, 0)),
                  pl.BlockSpec((k, d), lambda i: (0, 0)),
                  pl.BlockSpec((tm, d), lambda i: (i, 0)), vec, vec],
        out_specs=pl.BlockSpec((tm, d), lambda i: (i, 0)),
        out_shape=jax.ShapeDtypeStruct((t, d), F32),
        compiler_params=_cparams(("parallel",)),
        name="proj_ln",
    )(a, w, x2, g, b)


def _mlp_ln_kernel(x_ref, wu_ref, wd_ref, g_ref, b_ref, o_ref, xb_ref, acc_ref, *, alpha):
    f = pl.program_id(1)

    @pl.when(f == 0)
    def _():
        xb_ref[...] = x_ref[...].astype(BF16)
        acc_ref[...] = jnp.zeros_like(acc_ref)

    h = jnp.dot(xb_ref[...], wu_ref[...], preferred_element_type=F32)
    h = jnp.maximum(h, 0.0)
    h = h * h
    acc_ref[...] += jnp.dot(h.astype(BF16), wd_ref[...], preferred_element_type=F32)

    @pl.when(f == pl.num_programs(1) - 1)
    def _():
        o_ref[...] = _layer_norm_rows(alpha * x_ref[...] + acc_ref[...], g_ref[...], b_ref[...])


def _mlp_ln(x2, w_up, w_down, g, b, alpha, tm, tf):
    t, d = x2.shape
    ff = w_up.shape[1]
    tm = min(tm, t)
    tf = min(tf, ff)
    vec = pl.BlockSpec((1, d), lambda i, f: (0, 0))
    return pl.pallas_call(
        functools.partial(_mlp_ln_kernel, alpha=alpha),
        grid=(t // tm, ff // tf),
        in_specs=[pl.BlockSpec((tm, d), lambda i, f: (i, 0)),
                  pl.BlockSpec((d, tf), lambda i, f: (0, f)),
                  pl.BlockSpec((tf, d), lambda i, f: (f, 0)), vec, vec],
        out_specs=pl.BlockSpec((tm, d), lambda i, f: (i, 0)),
        out_shape=jax.ShapeDtypeStruct((t, d), F32),
        scratch_shapes=[pltpu.VMEM((tm, d), BF16), pltpu.VMEM((tm, d), F32)],
        compiler_params=_cparams(("parallel", "arbitrary")),
        name="mlp_ln",
    )(x2, w_up, w_down, g, b)


def _diff_attn_kernel(q_ref, k_ref, v_ref, lam_ref, sw_ref, o_ref, qs_ref, s_ref, acc_ref,
                      *, tq, tk, nh, lambda_init):
    i = pl.program_id(2)
    heads = list(range(nh))
    hs = [slice(h * HEAD_W, (h + 1) * HEAD_W) for h in heads]
    half = HEAD_W // 2
    lane = lax.broadcasted_iota(jnp.int32, (tq, HEAD_W), 1)
    for h in heads:
        q = q_ref[:, hs[h]]
        zero = jnp.zeros_like(q)
        qs_ref[h, 0:tq, :] = jnp.where(lane < half, q, zero)
        qs_ref[h, tq:2 * tq, :] = jnp.where(lane >= half, q, zero)
    acc_ref[...] = jnp.zeros_like(acc_ref)
    cw = min(V7X_MXU_COLS, tq)
    assert tq % cw == 0
    ones_rows = jnp.ones((ONES_ROWS, tk), BF16)

    def produce(slot, j):
        rows = pl.ds(pl.multiple_of(j * tk, tk), tk)
        for h in heads:
            s_ref[h, slot] = lax.dot_general(k_ref[rows, hs[h]], qs_ref[h], (((1,), (1,)), ((), ())),
                                             preferred_element_type=F32)

    def consume(slot, j, m_prev, diag):
        rows = pl.ds(pl.multiple_of(j * tk, tk), tk)
        lhs = [jnp.concatenate([v_ref[rows, hs[h]].T, ones_rows], axis=0) for h in heads]
        m_parts = [[] for _ in heads]
        for c in range(2 * tq // cw):
            cs = slice(c * cw, (c + 1) * cw)
            r_min = (c * cw) % tq
            if diag is not None and diag * tk > r_min + cw - 1:
                for h in heads:
                    m_parts[h].append(m_prev[h][:, cs])
                continue
            for h in heads:
                m_prev_c = m_prev[h][:, cs]
                sc = s_ref[h, slot, :, cs]
                if diag is not None and diag * tk + tk - 1 > r_min:
                    key = lax.broadcasted_iota(jnp.int32, (tk, cw), 0) + diag * tk
                    qry = lax.broadcasted_iota(jnp.int32, (tk, cw), 1) + r_min
                    sc = jnp.where(key <= qry, sc, -jnp.inf)
                m_new_c = jnp.maximum(m_prev_c, jnp.max(sc, axis=0, keepdims=True))
                alpha_c = jnp.exp2(m_prev_c - m_new_c)
                p_c = jnp.exp2(sc - m_new_c).astype(BF16)
                pv_c = jnp.dot(lhs[h], p_c, preferred_element_type=F32)
                acc_ref[h, :, cs] = alpha_c * acc_ref[h, :, cs] + pv_c
                m_parts[h].append(m_new_c)
        return tuple(jnp.concatenate(parts, axis=1) for parts in m_parts)

    assert tq == 2 * tk

    def pair(jj, m):
        produce(1, 2 * jj + 1)
        m = consume(0, 2 * jj, m, None)
        produce(0, 2 * jj + 2)
        return consume(1, 2 * jj + 1, m, None)

    produce(0, 0)
    m = lax.fori_loop(0, i, pair, tuple(jnp.full((1, 2 * tq), -jnp.inf, F32) for _ in heads))
    produce(1, 2 * i + 1)
    m = consume(0, 2 * i, m, 0)
    m = consume(1, 2 * i + 1, m, 1)

    lp = lam_ref[...]
    lam = (jnp.exp(jnp.sum(lp[0:1, :] * lp[1:2, :], axis=-1, keepdims=True))
           - jnp.exp(jnp.sum(lp[2:3, :] * lp[3:4, :], axis=-1, keepdims=True)) + lambda_init)
    o_all = [acc_ref[h, 0:HEAD_W, :] * (1.0 / acc_ref[h, HEAD_W:HEAD_W + 1, :]) for h in heads]
    o = [(a[:, 0:tq] - lam * a[:, tq:2 * tq]).T for a in o_all]
    o = [a * lax.rsqrt(jnp.mean(a * a, axis=-1, keepdims=True) + SUBLN_EPS) * sw_ref[...] for a in o]
    for h in heads:
        o_ref[:, hs[h]] = (o[h] * (1.0 - lambda_init)).astype(o_ref.dtype)


def _diff_attention(q3, kv3, lam_params, subln_w, lambda_init, tq, tk, nh):
    b, s, wq = q3.shape
    heads = wq // HEAD_W
    tq = min(tq, s)
    tk = min(tk, tq)
    nh = min(nh, heads)
    groups = heads // nh
    wb = nh * HEAD_W
    return pl.pallas_call(
        functools.partial(_diff_attn_kernel, tq=tq, tk=tk, nh=nh, lambda_init=lambda_init),
        grid=(b, groups, s // tq),
        in_specs=[pl.BlockSpec((None, tq, wb), lambda bi, g, i: (bi, i, g)),
                  pl.BlockSpec((None, s, wb), lambda bi, g, i: (bi, 0, g)),
                  pl.BlockSpec((None, s, wb), lambda bi, g, i, groups=groups: (bi, 0, groups + g)),
                  pl.BlockSpec(lam_params.shape, lambda bi, g, i: (0, 0)),
                  pl.BlockSpec((1, HEAD_W), lambda bi, g, i: (0, 0))],
        out_specs=pl.BlockSpec((None, tq, wb), lambda bi, g, i: (bi, i, g)),
        out_shape=jax.ShapeDtypeStruct((b, s, wq), BF16),
        scratch_shapes=[pltpu.VMEM((nh, 2 * tq, HEAD_W), BF16),
                        pltpu.VMEM((nh, 2, tk, 2 * tq), F32),
                        pltpu.VMEM((nh, HEAD_W + ONES_ROWS, 2 * tq), F32)],
        compiler_params=_cparams(("parallel", "parallel", "arbitrary")),
        name="diff_attn",
    )(q3, kv3, kv3, lam_params, subln_w)


def _gdn_layer(x2, bsz, seq, w_in, conv_w, a_log, dt_bias, norm_w, w_out, ln_g, ln_b, alpha, hb, tiles):
    t, d = x2.shape
    heads = a_log.shape[0]
    v_w = heads * HEAD_W
    groups = heads // hb
    main = 4 * v_w
    p = _matmul(x2, w_in[:, :main].astype(BF16), BF16, tiles["tm"], tiles["tn"], name="gdn_in_proj")
    wb = w_in[:, main:main + heads].reshape(d, groups, hb)
    wa = w_in[:, main + heads:main + 2 * heads].reshape(d, groups, hb)
    w_gate = jnp.concatenate([wb, wa], axis=2).reshape(d, 2 * heads)
    w_gate = jnp.pad(w_gate, ((0, 0), (0, GATE_W - 2 * heads)))

    def per_col(vals, fill):
        zeros = jnp.full((groups, hb), fill, F32)
        cols = jnp.concatenate([zeros, vals.astype(F32).reshape(groups, hb)], axis=1).reshape(1, 2 * heads)
        return jnp.pad(cols, ((0, 0), (0, GATE_W - 2 * heads)), constant_values=fill)

    is_a = per_col(jnp.ones((heads,), F32), 0.0)
    neg_a = per_col(-jnp.exp(a_log.astype(F32)), 0.0)
    dtb = per_col(dt_bias, 0.0)
    gates = _gates(x2, w_gate, is_a, neg_a, dtb, tiles["tm_gate"])[:, :2 * heads]
    gcol = gates.reshape(bsz, seq, groups, 2 * hb).transpose(0, 2, 1, 3)
    grow = gates.reshape(bsz, seq // CHUNK, CHUNK, groups, 2 * hb).transpose(0, 3, 1, 4, 2)
    o = _gdn_core(p.reshape(bsz, seq, main), conv_w, gcol, grow, norm_w.reshape(1, HEAD_W), hb,
                  tiles["nc"])
    return _proj_ln(o.reshape(t, v_w), w_out.astype(BF16), x2, ln_g.reshape(1, d), ln_b.reshape(1, d),
                    alpha, tiles["tm_proj"])


def _tiles():
    return dict(tm=1024, tn=1024, tm_gate=512, tm_proj=512, tm_mlp=512, tf_mlp=1024, tq=512, tk=256, nh=2, hb=16, nc=1)


def _forward(x, gdn_w_in, gdn_conv_w, gdn_a_log, gdn_dt_bias, gdn_norm_w, gdn_w_out, diff_w_q, diff_lambda,
             diff_subln_w, diff_w_o, shared_w_kv, mlp_w_up, mlp_w_down, ln_g, ln_b, tiles):
    bsz, seq, d = x.shape
    t = bsz * seq
    depth = mlp_w_up.shape[0]
    n_a = gdn_w_in.shape[0]
    alpha = (2 * depth) ** 0.25
    x2 = x.reshape(t, d).astype(F32)
    kv3 = None
    for l in range(depth):
        if l < n_a:
            x2 = _gdn_layer(x2, bsz, seq, gdn_w_in[l], gdn_conv_w[l], gdn_a_log[l], gdn_dt_bias[l],
                            gdn_norm_w[l], gdn_w_out[l], ln_g[l, 0], ln_b[l, 0], alpha,
                            min(tiles["hb"], gdn_a_log.shape[1]), tiles)
        else:
            j = l - n_a
            lambda_init = 0.8 - 0.6 * math.exp(-0.3 * l)
            dk = diff_lambda.shape[-1]
            q = _matmul(x2, diff_w_q[j].astype(BF16), BF16, tiles["tm"], tiles["tn"],
                        scale=dk ** -0.5 * math.log2(math.e), name="diff_q_proj")
            o = _diff_attention(q.reshape(bsz, seq, -1), kv3, diff_lambda[j].astype(F32),
                                diff_subln_w[j].reshape(1, HEAD_W).astype(F32), lambda_init, tiles["tq"],
                                tiles["tk"], tiles["nh"])
            x2 = _proj_ln(o.reshape(t, -1), diff_w_o[j].astype(BF16), x2, ln_g[l, 0].reshape(1, d),
                          ln_b[l, 0].reshape(1, d), alpha, tiles["tm_proj"])
        x2 = _mlp_ln(x2, mlp_w_up[l].astype(BF16), mlp_w_down[l].astype(BF16), ln_g[l, 1].reshape(1, d),
                     ln_b[l, 1].reshape(1, d), alpha, tiles["tm_mlp"], tiles["tf_mlp"])
        if l == n_a - 1:
            kv = _matmul(x2, shared_w_kv.astype(BF16), BF16, tiles["tm"], tiles["tn"], name="shared_kv_proj")
            kv3 = kv.reshape(bsz, seq, -1)
    return x2.reshape(bsz, seq, d).astype(x.dtype)


def kernel(x, gdn_w_in, gdn_conv_w, gdn_a_log, gdn_dt_bias, gdn_norm_w, gdn_w_out, diff_w_q, diff_lambda,
           diff_subln_w, diff_w_o, shared_w_kv, mlp_w_up, mlp_w_down, ln_g, ln_b):
    return _forward(x, gdn_w_in, gdn_conv_w, gdn_a_log, gdn_dt_bias, gdn_norm_w, gdn_w_out, diff_w_q,
                    diff_lambda, diff_subln_w, diff_w_o, shared_w_kv, mlp_w_up, mlp_w_down, ln_g, ln_b,
                    _tiles())
```

```python
import functools
import math

import jax
import jax.numpy as jnp
from jax import lax
from jax.experimental import pallas as pl
from jax.experimental.pallas import tpu as pltpu

F32 = jnp.float32
BF16 = jnp.bfloat16

HEAD_W = 128
CHUNK = 64
CONV_TAPS = 4
GDN_EPS = 1e-6
SUBLN_EPS = 1e-5
LN_EPS = 1e-5
GATE_W = 128
LN_ROW_CHUNKS = 4

V7X_SUBLANES = 8
V7X_MXU_COLS = 256
ONES_ROWS = 16
V7X_VMEM_LIMIT_BYTES = 56 * 1024 * 1024


def _cparams(sem):
    return pltpu.CompilerParams(dimension_semantics=sem, vmem_limit_bytes=V7X_VMEM_LIMIT_BYTES)


def _bdot(a, b):
    return jnp.dot(a.astype(BF16), b.astype(BF16), preferred_element_type=F32)


def _bdot_nt(a, b):
    return lax.dot_general(a.astype(BF16), b.astype(BF16), (((1,), (1,)), ((), ())),
                           preferred_element_type=F32)


def _bdot_tn(a, b):
    return lax.dot_general(a.astype(BF16), b.astype(BF16), (((0,), (0,)), ((), ())),
                           preferred_element_type=F32)


def _split3(x):
    h = x.astype(BF16)
    r = x - h.astype(F32)
    m = r.astype(BF16)
    l = (r - m.astype(F32)).astype(BF16)
    return h, m, l


def _mm_kernel(x_ref, w_ref, o_ref, *, scale):
    acc = jnp.dot(x_ref[...].astype(BF16), w_ref[...], preferred_element_type=F32)
    if scale != 1.0:
        acc = acc * scale
    o_ref[...] = acc.astype(o_ref.dtype)


def _matmul(x, w, out_dtype, tm, tn, scale=1.0, name="matmul", n_cols=None):
    m, k = x.shape
    n = w.shape[1] if n_cols is None else n_cols
    tm = min(tm, m)
    tn = min(tn, n)
    assert m % tm == 0 and n % tn == 0
    return pl.pallas_call(
        functools.partial(_mm_kernel, scale=scale),
        grid=(m // tm, n // tn),
        in_specs=[pl.BlockSpec((tm, k), lambda i, j: (i, 0)),
                  pl.BlockSpec((k, tn), lambda i, j: (0, j))],
        out_specs=pl.BlockSpec((tm, tn), lambda i, j: (i, j)),
        out_shape=jax.ShapeDtypeStruct((m, n), out_dtype),
        compiler_params=_cparams(("parallel", "parallel")),
        name=name,
    )(x, w)


def _gate_kernel(x_ref, w_ref, isa_ref, nega_ref, dtb_ref, o_ref):
    x = x_ref[...]
    w = w_ref[...]
    xh = x.astype(BF16)
    xl = (x - xh.astype(F32)).astype(BF16)
    wh = w.astype(BF16)
    wl = (w - wh.astype(F32)).astype(BF16)
    acc = (jnp.dot(xh, wh, preferred_element_type=F32)
           + jnp.dot(xl, wh, preferred_element_type=F32)
           + jnp.dot(xh, wl, preferred_element_type=F32))
    beta = jax.nn.sigmoid(acc)
    s = acc + dtb_ref[...]
    softplus = jnp.maximum(s, 0.0) + jnp.log1p(jnp.exp(-jnp.abs(s)))
    g = nega_ref[...] * softplus
    o_ref[...] = jnp.where(isa_ref[...] > 0.5, g, beta)


def _gates(x2, w_gate, is_a, neg_a, dtb, tm):
    t, d = x2.shape
    tm = min(tm, t)
    vec = pl.BlockSpec((1, GATE_W), lambda i: (0, 0))
    return pl.pallas_call(
        _gate_kernel,
        grid=(t // tm,),
        in_specs=[pl.BlockSpec((tm, d), lambda i: (i, 0)),
                  pl.BlockSpec((d, GATE_W), lambda i: (0, 0)), vec, vec, vec],
        out_specs=pl.BlockSpec((tm, GATE_W), lambda i: (i, 0)),
        out_shape=jax.ShapeDtypeStruct((t, GATE_W), F32),
        compiler_params=_cparams(("parallel",)),
        name="gdn_gates",
    )(x2, w_gate, is_a, neg_a, dtb)


def _each(fn, *lists):
    return [fn(*args) for args in zip(*lists)]


def _unit_lower_inverse_minus_identity(lows, same16, same32):
    c = lows[0].shape[0]
    d = _each(lambda low: jnp.where(same16, low, 0.0), lows)
    x = _each(lambda a: -a, d)
    p = _each(lambda a: _bdot(a, a), d)
    xp = _each(lambda a, b: _bdot(jnp.concatenate([a, b], axis=0), b), x, p)
    n = _each(lambda a, b, ab: a + b + ab[:c], x, p, xp)
    p2 = _each(lambda ab: ab[c:], xp)
    np2 = _each(lambda a, b: _bdot(jnp.concatenate([a, b], axis=0), b), n, p2)
    n = _each(lambda a, b, ab: a + b + ab[:c], n, p2, np2)
    p4 = _each(lambda ab: ab[c:], np2)
    n = _each(lambda a, b: a + b + _bdot(a, b), n, p4)
    only32 = jnp.logical_and(same32, jnp.logical_not(same16))
    off1 = _each(lambda low: jnp.where(only32, low, 0.0), lows)
    c1 = _each(lambda a, o: o + _bdot(a, o), n, off1)
    n = _each(lambda a, cc: a - (cc + _bdot(cc, a)), n, c1)
    off2 = _each(lambda low: jnp.where(same32, 0.0, low), lows)
    c2 = _each(lambda a, o: o + _bdot(a, o), n, off2)
    n = _each(lambda a, cc: a - (cc + _bdot(cc, a)), n, c2)
    return n


def _gdn_kernel(pq_ref, pk_ref, pv_ref, pz_ref, cwq_ref, cwk_ref, cwv_ref, gcol_ref, grow_ref,
                nw_ref, o_ref, hq_ref, hk_ref, hv_ref, state_ref, *, hb, nc):
    c_idx = pl.program_id(2)
    C = CHUNK
    w_blk = hb * HEAD_W
    chunks = list(range(nc))
    rows = [slice(cc * C, (cc + 1) * C) for cc in chunks]

    @pl.when(c_idx == 0)
    def _():
        state_ref[...] = jnp.zeros_like(state_ref)
        zero = jnp.zeros((C, w_blk), BF16)
        hq_ref[0:C, :] = zero
        hk_ref[0:C, :] = zero
        hv_ref[0:C, :] = zero

    sr = lax.broadcasted_iota(jnp.int32, ((CONV_TAPS - 1) * C, 2 * C), 0)
    sc = lax.broadcasted_iota(jnp.int32, ((CONV_TAPS - 1) * C, 2 * C), 1)
    tap = sr // C
    shift_mat = jnp.where(sc == sr - tap * (C - 1) + (C - (CONV_TAPS - 1)), 1.0, 0.0).astype(BF16)

    def conv_silu(p_ref, h_ref, cw_ref):
        h_ref[C:(nc + 1) * C, :] = p_ref[...]
        w = cw_ref[...]
        out = []
        for cc in chunks:
            window = h_ref[cc * C:(cc + 2) * C, :]
            shifted = jnp.dot(shift_mat, window, preferred_element_type=F32)
            y = shifted[0:C] * w[0:1, :]
            for j in range(1, CONV_TAPS - 1):
                y = y + shifted[j * C:(j + 1) * C] * w[j:j + 1, :]
            y = y + window[C:2 * C].astype(F32) * w[CONV_TAPS - 1:CONV_TAPS, :]
            out.append(y * jax.nn.sigmoid(y))
        h_ref[0:C, :] = h_ref[nc * C:(nc + 1) * C, :]
        return out

    yq = conv_silu(pq_ref, hq_ref, cwq_ref)
    yk = conv_silu(pk_ref, hk_ref, cwk_ref)
    yv = conv_silu(pv_ref, hv_ref, cwv_ref)

    row = lax.broadcasted_iota(jnp.int32, (C, C), 0)
    col = lax.broadcasted_iota(jnp.int32, (C, C), 1)
    causal = row >= col
    strict = row > col
    same16 = (row // 16) == (col // 16)
    same32 = (row // 32) == (col // 32)
    tril = jnp.where(causal, 1.0, 0.0).astype(BF16)
    triu = jnp.where(row <= col, 1.0, 0.0).astype(BF16)

    gcol = gcol_ref[...]
    gc_cols, gc_rows, grows = [], [], []
    for cc in chunks:
        gh, gm, gl = _split3(gcol[rows[cc], hb:])
        gc_cols.append(jnp.dot(tril, gh, preferred_element_type=F32)
                       + jnp.dot(tril, gm, preferred_element_type=F32)
                       + jnp.dot(tril, gl, preferred_element_type=F32))
        grow = grow_ref[cc]
        rh, rm, rl = _split3(grow[hb:, :])
        gc_rows.append(jnp.dot(rh, triu, preferred_element_type=F32)
                       + jnp.dot(rm, triu, preferred_element_type=F32)
                       + jnp.dot(rl, triu, preferred_element_type=F32))

    nw = nw_ref[...]
    q_scale = HEAD_W ** -0.5

    heads = list(range(hb))
    sl = [slice(h * HEAD_W, (h + 1) * HEAD_W) for h in heads]
    items = [(cc, h) for cc in chunks for h in heads]

    def l2n(y, scale):
        return y * (lax.rsqrt(jnp.sum(y * y, axis=-1, keepdims=True) + GDN_EPS) * scale)

    q = [l2n(yq[cc][:, sl[h]], q_scale) for cc, h in items]
    k = [l2n(yk[cc][:, sl[h]], 1.0) for cc, h in items]
    v = [yv[cc][:, sl[h]] for cc, h in items]
    beta = [gcol[rows[cc], h:h + 1] for cc, h in items]
    g_c = [gc_cols[cc][:, h:h + 1] for cc, h in items]
    g_r = [gc_rows[cc][h:h + 1, :] for cc, h in items]
    g_last = [r[:, C - 1:C] for r in g_r]
    decay = _each(lambda gc, gr: jnp.exp(jnp.where(causal, gc - gr, -jnp.inf)), g_c, g_r)
    eg = _each(jnp.exp, g_c)
    k_beta = _each(lambda a, b: a * b, k, beta)
    kq = _each(lambda kb, qq, kk: _bdot_nt(jnp.concatenate([kb, qq], axis=0), kk), k_beta, q, k)
    low = _each(lambda a, dc: jnp.where(strict, a[:C] * dc, 0.0), kq, decay)
    attn = _each(lambda a, dc: a[C:] * dc, kq, decay)
    n_inv = _unit_lower_inverse_minus_identity(low, same16, same32)
    rhs = _each(lambda vv, b, kb, e: jnp.concatenate([vv * b, kb * e], axis=1), v, beta, k_beta, eg)
    sol = _each(lambda r, n: r + _bdot(n, r), rhs, n_inv)
    wq = _each(lambda s, qq, e: jnp.concatenate([s[:, HEAD_W:], qq * e], axis=0), sol, q, eg)
    k_dec = _each(lambda kk, gl_, gc: kk * jnp.exp(gl_ - gc), k, g_last, g_c)
    e_last = _each(jnp.exp, g_last)

    state = [state_ref[h] for h in heads]
    o = []
    for cc in chunks:
        it = slice(cc * hb, (cc + 1) * hb)
        ws = _each(_bdot, wq[it], state)
        v_new = _each(lambda s, a: s[:, :HEAD_W] - a[:C], sol[it], ws)
        o += _each(lambda a, at, vn: a[C:] + _bdot(at, vn), ws, attn[it], v_new)
        state = _each(lambda st, el, kd, vn: st * el + _bdot_tn(kd, vn), state, e_last[it], k_dec[it], v_new)
    for h in heads:
        state_ref[h] = state[h]
    o = _each(lambda a: a * lax.rsqrt(jnp.mean(a * a, axis=-1, keepdims=True) + GDN_EPS) * nw, o)
    for idx, (cc, h) in enumerate(items):
        zh = pz_ref[rows[cc], sl[h]].astype(F32)
        o_ref[rows[cc], sl[h]] = (o[idx] * (zh * jax.nn.sigmoid(zh))).astype(o_ref.dtype)


def _gdn_core(p3, conv_w, gcol, grow, norm_w, hb, nc):
    b, s, w4 = p3.shape
    assert p3.dtype == BF16
    v_w = w4 // 4
    heads = v_w // HEAD_W
    groups = heads // hb
    wb = hb * HEAD_W
    tc = nc * CHUNK
    assert s % tc == 0

    def pspec(part):
        return pl.BlockSpec((None, tc, wb), lambda bi, gi, ci, part=part: (bi, ci, part * groups + gi))

    def cspec(part):
        return pl.BlockSpec((CONV_TAPS, wb), lambda bi, gi, ci, part=part: (0, part * groups + gi))

    halo = pltpu.VMEM(((nc + 1) * CHUNK, wb), BF16)
    return pl.pallas_call(
        functools.partial(_gdn_kernel, hb=hb, nc=nc),
        grid=(b, groups, s // tc),
        in_specs=[pspec(0), pspec(1), pspec(2), pspec(3), cspec(0), cspec(1), cspec(2),
                  pl.BlockSpec((None, None, tc, 2 * hb), lambda bi, gi, ci: (bi, gi, ci, 0)),
                  pl.BlockSpec((None, None, nc, 2 * hb, CHUNK), lambda bi, gi, ci: (bi, gi, ci, 0, 0)),
                  pl.BlockSpec((1, HEAD_W), lambda bi, gi, ci: (0, 0))],
        out_specs=pl.BlockSpec((None, tc, wb), lambda bi, gi, ci: (bi, ci, gi)),
        out_shape=jax.ShapeDtypeStruct((b, s, v_w), BF16),
        scratch_shapes=[halo, halo, halo, pltpu.VMEM((hb, HEAD_W, HEAD_W), F32)],
        compiler_params=_cparams(("parallel", "parallel", "arbitrary")),
        name="gdn_core",
    )(p3, p3, p3, p3, conv_w, conv_w, conv_w, gcol, grow, norm_w)


def _layer_norm_rows(y, g, b):
    mu = jnp.mean(y, axis=-1, keepdims=True)
    yc = y - mu
    var = jnp.mean(yc * yc, axis=-1, keepdims=True)
    return yc * lax.rsqrt(var + LN_EPS) * g + b


def _row_chunks(rows):
    n = LN_ROW_CHUNKS if rows % (LN_ROW_CHUNKS * V7X_SUBLANES) == 0 else 1
    step = rows // n
    return [slice(k * step, (k + 1) * step) for k in range(n)]


def _matmul_then_ln(chunks, matmul_rows, ln_store_rows):
    pending = None
    for rs in chunks:
        h = matmul_rows(rs)
        if pending is not None:
            ln_store_rows(*pending)
        pending = (rs, h)
    ln_store_rows(*pending)


def _proj_ln_kernel(a_ref, w_ref, x_ref, g_ref, b_ref, o_ref, *, alpha):
    def matmul_rows(rs):
        return jnp.dot(a_ref[rs, :].astype(BF16), w_ref[...], preferred_element_type=F32)

    def ln_store_rows(rs, h):
        o_ref[rs, :] = _layer_norm_rows(alpha * x_ref[rs, :] + h, g_ref[...], b_ref[...])

    _matmul_then_ln(_row_chunks(a_ref.shape[0]), matmul_rows, ln_store_rows)


def _proj_ln(a, w, x2, g, b, alpha, tm):
    t, k = a.shape
    d = w.shape[1]
    tm = min(tm, t)
    vec = pl.BlockSpec((1, d), lambda i: (0, 0))
    return pl.pallas_call(
        functools.partial(_proj_ln_kernel, alpha=alpha),
        grid=(t // tm,),
        in_specs=[pl.BlockSpec((tm, k), lambda i: (i, 0)),
                  pl.BlockSpec((k, d), lambda i: (0, 0)),
                  pl.BlockSpec((tm, d), lambda i: (i, 0)), vec, vec],
        out_specs=pl.BlockSpec((tm, d), lambda i: (i, 0)),
        out_shape=jax.ShapeDtypeStruct((t, d), F32),
        compiler_params=_cparams(("parallel",)),
        name="proj_ln",
    )(a, w, x2, g, b)


def _mlp_ln_kernel(x_ref, wu_ref, wd_ref, g_ref, b_ref, o_ref, xb_ref, acc_ref, *, alpha):
    f = pl.program_id(1)

    @pl.when(f == 0)
    def _():
        xb_ref[...] = x_ref[...].astype(BF16)
        acc_ref[...] = jnp.zeros_like(acc_ref)

    h = jnp.dot(xb_ref[...], wu_ref[...], preferred_element_type=F32)
    h = jnp.maximum(h, 0.0)
    h = h * h
    acc_ref[...] += jnp.dot(h.astype(BF16), wd_ref[...], preferred_element_type=F32)

    @pl.when(f == pl.num_programs(1) - 1)
    def _():
        o_ref[...] = _layer_norm_rows(alpha * x_ref[...] + acc_ref[...], g_ref[...], b_ref[...])


def _mlp_ln(x2, w_up, w_down, g, b, alpha, tm, tf):
    t, d = x2.shape
    ff = w_up.shape[1]
    tm = min(tm, t)
    tf = min(tf, ff)
    vec = pl.BlockSpec((1, d), lambda i, f: (0, 0))
    return pl.pallas_call(
        functools.partial(_mlp_ln_kernel, alpha=alpha),
        grid=(t // tm, ff // tf),
        in_specs=[pl.BlockSpec((tm, d), lambda i, f: (i, 0)),
                  pl.BlockSpec((d, tf), lambda i, f: (0, f)),
                  pl.BlockSpec((tf, d), lambda i, f: (f, 0)), vec, vec],
        out_specs=pl.BlockSpec((tm, d), lambda i, f: (i, 0)),
        out_shape=jax.ShapeDtypeStruct((t, d), F32),
        scratch_shapes=[pltpu.VMEM((tm, d), BF16), pltpu.VMEM((tm, d), F32)],
        compiler_params=_cparams(("parallel", "arbitrary")),
        name="mlp_ln",
    )(x2, w_up, w_down, g, b)


def _diff_attn_kernel(q_ref, k_ref, v_ref, lam_ref, sw_ref, o_ref, qs_ref, s_ref, acc_ref,
                      *, tq, tk, nh, lambda_init):
    i = pl.program_id(2)
    heads = list(range(nh))
    hs = [slice(h * HEAD_W, (h + 1) * HEAD_W) for h in heads]
    half = HEAD_W // 2
    lane = lax.broadcasted_iota(jnp.int32, (tq, HEAD_W), 1)
    for h in heads:
        q = q_ref[:, hs[h]]
        zero = jnp.zeros_like(q)
        qs_ref[h, 0:tq, :] = jnp.where(lane < half, q, zero)
        qs_ref[h, tq:2 * tq, :] = jnp.where(lane >= half, q, zero)
    acc_ref[...] = jnp.zeros_like(acc_ref)
    cw = min(V7X_MXU_COLS, tq)
    assert tq % cw == 0
    ones_rows = jnp.ones((ONES_ROWS, tk), BF16)

    def produce(slot, j, diag=None):
        rows = pl.ds(pl.multiple_of(j * tk, tk), tk)
        first = 0 if diag is None else diag * tk
        for h in heads:
            kb = k_ref[rows, hs[h]]
            for lo, hi in ([(0, 2 * tq)] if first == 0 else [(first, tq), (tq + first, 2 * tq)]):
                s_ref[h, slot, :, lo:hi] = lax.dot_general(kb, qs_ref[h, lo:hi, :], (((1,), (1,)), ((), ())),
                                                           preferred_element_type=F32)

    def consume(slot, j, m_prev, diag):
        rows = pl.ds(pl.multiple_of(j * tk, tk), tk)
        lhs = [jnp.concatenate([v_ref[rows, hs[h]].T, ones_rows], axis=0) for h in heads]
        m_parts = [[] for _ in heads]
        for c in range(2 * tq // cw):
            cs = slice(c * cw, (c + 1) * cw)
            r_min = (c * cw) % tq
            if diag is not None and diag * tk > r_min + cw - 1:
                for h in heads:
                    m_parts[h].append(m_prev[h][:, cs])
                continue
            nk = tk if diag is None else min(tk, r_min + cw - diag * tk)
            for h in heads:
                m_prev_c = m_prev[h][:, cs]
                sc = s_ref[h, slot, 0:nk, cs]
                if diag is not None and diag * tk + nk - 1 > r_min:
                    key = lax.broadcasted_iota(jnp.int32, (nk, cw), 0) + diag * tk
                    qry = lax.broadcasted_iota(jnp.int32, (nk, cw), 1) + r_min
                    sc = jnp.where(key <= qry, sc, -jnp.inf)
                m_new_c = jnp.maximum(m_prev_c, jnp.max(sc, axis=0, keepdims=True))
                alpha_c = jnp.exp2(m_prev_c - m_new_c)
                p_c = jnp.exp2(sc - m_new_c).astype(BF16)
                pv_c = jnp.dot(lhs[h][:, 0:nk], p_c, preferred_element_type=F32)
                acc_ref[h, :, cs] = alpha_c * acc_ref[h, :, cs] + pv_c
                m_parts[h].append(m_new_c)
        return tuple(jnp.concatenate(parts, axis=1) for parts in m_parts)

    assert tq == 2 * tk

    def pair(jj, m):
        produce(1, 2 * jj + 1)
        m = consume(0, 2 * jj, m, None)
        produce(0, 2 * jj + 2)
        return consume(1, 2 * jj + 1, m, None)

    produce(0, 0)
    m = lax.fori_loop(0, i, pair, tuple(jnp.full((1, 2 * tq), -jnp.inf, F32) for _ in heads))
    produce(1, 2 * i + 1, 1)
    m = consume(0, 2 * i, m, 0)
    m = consume(1, 2 * i + 1, m, 1)

    lp = lam_ref[...]
    lam = (jnp.exp(jnp.sum(lp[0:1, :] * lp[1:2, :], axis=-1, keepdims=True))
           - jnp.exp(jnp.sum(lp[2:3, :] * lp[3:4, :], axis=-1, keepdims=True)) + lambda_init)
    o_all = [acc_ref[h, 0:HEAD_W, :] * (1.0 / acc_ref[h, HEAD_W:HEAD_W + 1, :]) for h in heads]
    o = [(a[:, 0:tq] - lam * a[:, tq:2 * tq]).T for a in o_all]
    o = [a * lax.rsqrt(jnp.mean(a * a, axis=-1, keepdims=True) + SUBLN_EPS) * sw_ref[...] for a in o]
    for h in heads:
        o_ref[:, hs[h]] = (o[h] * (1.0 - lambda_init)).astype(o_ref.dtype)


def _diff_attention(q3, kv3, lam_params, subln_w, lambda_init, tq, tk, nh):
    b, s, wq = q3.shape
    heads = wq // HEAD_W
    tq = min(tq, s)
    tk = min(tk, tq)
    nh = min(nh, heads)
    groups = heads // nh
    wb = nh * HEAD_W
    return pl.pallas_call(
        functools.partial(_diff_attn_kernel, tq=tq, tk=tk, nh=nh, lambda_init=lambda_init),
        grid=(b, groups, s // tq),
        in_specs=[pl.BlockSpec((None, tq, wb), lambda bi, g, i: (bi, i, g)),
                  pl.BlockSpec((None, s, wb), lambda bi, g, i: (bi, 0, g)),
                  pl.BlockSpec((None, s, wb), lambda bi, g, i, groups=groups: (bi, 0, groups + g)),
                  pl.BlockSpec(lam_params.shape, lambda bi, g, i: (0, 0)),
                  pl.BlockSpec((1, HEAD_W), lambda bi, g, i: (0, 0))],
        out_specs=pl.BlockSpec((None, tq, wb), lambda bi, g, i: (bi, i, g)),
        out_shape=jax.ShapeDtypeStruct((b, s, wq), BF16),
        scratch_shapes=[pltpu.VMEM((nh, 2 * tq, HEAD_W), BF16),
                        pltpu.VMEM((nh, 2, tk, 2 * tq), F32),
                        pltpu.VMEM((nh, HEAD_W + ONES_ROWS, 2 * tq), F32)],
        compiler_params=_cparams(("parallel", "parallel", "arbitrary")),
        name="diff_attn",
    )(q3, kv3, kv3, lam_params, subln_w)


def _gdn_layer(x2, bsz, seq, w_in, conv_w, a_log, dt_bias, norm_w, w_out, ln_g, ln_b, alpha, hb, tiles):
    t, d = x2.shape
    heads = a_log.shape[0]
    v_w = heads * HEAD_W
    groups = heads // hb
    main = 4 * v_w
    p = _matmul(x2, w_in.astype(BF16), BF16, tiles["tm"], tiles["tn"], name="gdn_in_proj", n_cols=main)
    wb = w_in[:, main:main + heads].reshape(d, groups, hb)
    wa = w_in[:, main + heads:main + 2 * heads].reshape(d, groups, hb)
    w_gate = jnp.concatenate([wb, wa], axis=2).reshape(d, 2 * heads)
    w_gate = jnp.pad(w_gate, ((0, 0), (0, GATE_W - 2 * heads)))

    def per_col(vals, fill):
        zeros = jnp.full((groups, hb), fill, F32)
        cols = jnp.concatenate([zeros, vals.astype(F32).reshape(groups, hb)], axis=1).reshape(1, 2 * heads)
        return jnp.pad(cols, ((0, 0), (0, GATE_W - 2 * heads)), constant_values=fill)

    is_a = per_col(jnp.ones((heads,), F32), 0.0)
    neg_a = per_col(-jnp.exp(a_log.astype(F32)), 0.0)
    dtb = per_col(dt_bias, 0.0)
    gates = _gates(x2, w_gate, is_a, neg_a, dtb, tiles["tm_gate"])[:, :2 * heads]
    gcol = gates.reshape(bsz, seq, groups, 2 * hb).transpose(0, 2, 1, 3)
    grow = gates.reshape(bsz, seq // CHUNK, CHUNK, groups, 2 * hb).transpose(0, 3, 1, 4, 2)
    o = _gdn_core(p.reshape(bsz, seq, main), conv_w, gcol, grow, norm_w.reshape(1, HEAD_W), hb,
                  tiles["nc"])
    return _proj_ln(o.reshape(t, v_w), w_out.astype(BF16), x2, ln_g.reshape(1, d), ln_b.reshape(1, d),
                    alpha, tiles["tm_proj"])


def _tiles():
    return dict(tm=1024, tn=1024, tm_gate=512, tm_proj=512, tm_mlp=512, tf_mlp=1024, tq=1024, tk=512, nh=2, hb=16, nc=1)


def _forward(x, gdn_w_in, gdn_conv_w, gdn_a_log, gdn_dt_bias, gdn_norm_w, gdn_w_out, diff_w_q, diff_lambda,
             diff_subln_w, diff_w_o, shared_w_kv, mlp_w_up, mlp_w_down, ln_g, ln_b, tiles):
    bsz, seq, d = x.shape
    t = bsz * seq
    depth = mlp_w_up.shape[0]
    n_a = gdn_w_in.shape[0]
    alpha = (2 * depth) ** 0.25
    x2 = x.reshape(t, d).astype(F32)
    kv3 = None
    for l in range(depth):
        if l < n_a:
            x2 = _gdn_layer(x2, bsz, seq, gdn_w_in[l], gdn_conv_w[l], gdn_a_log[l], gdn_dt_bias[l],
                            gdn_norm_w[l], gdn_w_out[l], ln_g[l, 0], ln_b[l, 0], alpha,
                            min(tiles["hb"], gdn_a_log.shape[1]), tiles)
        else:
            j = l - n_a
            lambda_init = 0.8 - 0.6 * math.exp(-0.3 * l)
            dk = diff_lambda.shape[-1]
            q = _matmul(x2, diff_w_q[j].astype(BF16), BF16, tiles["tm"], tiles["tn"],
                        scale=dk ** -0.5 * math.log2(math.e), name="diff_q_proj")
            o = _diff_attention(q.reshape(bsz, seq, -1), kv3, diff_lambda[j].astype(F32),
                                diff_subln_w[j].reshape(1, HEAD_W).astype(F32), lambda_init, tiles["tq"],
                                tiles["tk"], tiles["nh"])
            x2 = _proj_ln(o.reshape(t, -1), diff_w_o[j].astype(BF16), x2, ln_g[l, 0].reshape(1, d),
                          ln_b[l, 0].reshape(1, d), alpha, tiles["tm_proj"])
        x2 = _mlp_ln(x2, mlp_w_up[l].astype(BF16), mlp_w_down[l].astype(BF16), ln_g[l, 1].reshape(1, d),
                     ln_b[l, 1].reshape(1, d), alpha, tiles["tm_mlp"], tiles["tf_mlp"])
        if l == n_a - 1:
            kv = _matmul(x2, shared_w_kv.astype(BF16), BF16, tiles["tm"], tiles["tn"], name="shared_kv_proj")
            kv3 = kv.reshape(bsz, seq, -1)
    return x2.reshape(bsz, seq, d).astype(x.dtype)


def kernel(x, gdn_w_in, gdn_conv_w, gdn_a_log, gdn_dt_bias, gdn_norm_w, gdn_w_out, diff_w_q, diff_lambda,
           diff_subln_w, diff_w_o, shared_w_kv, mlp_w_up, mlp_w_down, ln_g, ln_b):
    return _forward(x, gdn_w_in, gdn_conv_w, gdn_a_log, gdn_dt_bias, gdn_norm_w, gdn_w_out, diff_w_q,
                    diff_lambda, diff_subln_w, diff_w_o, shared_w_kv, mlp_w_up, mlp_w_down, ln_g, ln_b,
                    _tiles())
```

```python
import functools
import math

import jax
import jax.numpy as jnp
from jax import lax
from jax.experimental import pallas as pl
from jax.experimental.pallas import tpu as pltpu

F32 = jnp.float32
BF16 = jnp.bfloat16

HEAD_W = 128
CHUNK = 64
CONV_TAPS = 4
GDN_EPS = 1e-6
SUBLN_EPS = 1e-5
LN_EPS = 1e-5
GATE_W = 128
LN_ROW_CHUNKS = 4

V7X_SUBLANES = 8
V7X_MXU_COLS = 256
ONES_ROWS = 16
V7X_VMEM_LIMIT_BYTES = 56 * 1024 * 1024


def _cparams(sem):
    return pltpu.CompilerParams(dimension_semantics=sem, vmem_limit_bytes=V7X_VMEM_LIMIT_BYTES)


def _bdot(a, b):
    return jnp.dot(a.astype(BF16), b.astype(BF16), preferred_element_type=F32)


def _bdot_nt(a, b):
    return lax.dot_general(a.astype(BF16), b.astype(BF16), (((1,), (1,)), ((), ())),
                           preferred_element_type=F32)


def _bdot_tn(a, b):
    return lax.dot_general(a.astype(BF16), b.astype(BF16), (((0,), (0,)), ((), ())),
                           preferred_element_type=F32)


def _split3(x):
    h = x.astype(BF16)
    r = x - h.astype(F32)
    m = r.astype(BF16)
    l = (r - m.astype(F32)).astype(BF16)
    return h, m, l


def _mm_kernel(x_ref, w_ref, o_ref, *, scale):
    acc = jnp.dot(x_ref[...].astype(BF16), w_ref[...], preferred_element_type=F32)
    if scale != 1.0:
        acc = acc * scale
    o_ref[...] = acc.astype(o_ref.dtype)


def _matmul(x, w, layer, out_dtype, tm, tn, scale=1.0, name="matmul", n_cols=None):
    m, k = x.shape
    n = w.shape[2] if n_cols is None else n_cols
    tm = min(tm, m)
    tn = min(tn, n)
    assert m % tm == 0 and n % tn == 0
    return pl.pallas_call(
        functools.partial(_mm_kernel, scale=scale),
        grid=(m // tm, n // tn),
        in_specs=[pl.BlockSpec((tm, k), lambda i, j: (i, 0)),
                  pl.BlockSpec((None, k, tn), lambda i, j: (layer, 0, j))],
        out_specs=pl.BlockSpec((tm, tn), lambda i, j: (i, j)),
        out_shape=jax.ShapeDtypeStruct((m, n), out_dtype),
        compiler_params=_cparams(("parallel", "parallel")),
        name=name,
    )(x, w)


def _gate_kernel(x_ref, w_ref, isa_ref, nega_ref, dtb_ref, o_ref):
    x = x_ref[...]
    w = w_ref[...]
    xh = x.astype(BF16)
    xl = (x - xh.astype(F32)).astype(BF16)
    wh = w.astype(BF16)
    wl = (w - wh.astype(F32)).astype(BF16)
    acc = (jnp.dot(xh, wh, preferred_element_type=F32)
           + jnp.dot(xl, wh, preferred_element_type=F32)
           + jnp.dot(xh, wl, preferred_element_type=F32))
    beta = jax.nn.sigmoid(acc)
    s = acc + dtb_ref[...]
    softplus = jnp.maximum(s, 0.0) + jnp.log1p(jnp.exp(-jnp.abs(s)))
    g = nega_ref[...] * softplus
    o_ref[...] = jnp.where(isa_ref[...] > 0.5, g, beta)


def _gates(x2, w_gate, is_a, neg_a, dtb, tm):
    t, d = x2.shape
    tm = min(tm, t)
    vec = pl.BlockSpec((1, GATE_W), lambda i: (0, 0))
    return pl.pallas_call(
        _gate_kernel,
        grid=(t // tm,),
        in_specs=[pl.BlockSpec((tm, d), lambda i: (i, 0)),
                  pl.BlockSpec((d, GATE_W), lambda i: (0, 0)), vec, vec, vec],
        out_specs=pl.BlockSpec((tm, GATE_W), lambda i: (i, 0)),
        out_shape=jax.ShapeDtypeStruct((t, GATE_W), F32),
        compiler_params=_cparams(("parallel",)),
        name="gdn_gates",
    )(x2, w_gate, is_a, neg_a, dtb)


def _each(fn, *lists):
    return [fn(*args) for args in zip(*lists)]


def _unit_lower_inverse_minus_identity(lows, same16, same32):
    c = lows[0].shape[0]
    d = _each(lambda low: jnp.where(same16, low, 0.0), lows)
    x = _each(lambda a: -a, d)
    p = _each(lambda a: _bdot(a, a), d)
    xp = _each(lambda a, b: _bdot(jnp.concatenate([a, b], axis=0), b), x, p)
    n = _each(lambda a, b, ab: a + b + ab[:c], x, p, xp)
    p2 = _each(lambda ab: ab[c:], xp)
    np2 = _each(lambda a, b: _bdot(jnp.concatenate([a, b], axis=0), b), n, p2)
    n = _each(lambda a, b, ab: a + b + ab[:c], n, p2, np2)
    p4 = _each(lambda ab: ab[c:], np2)
    n = _each(lambda a, b: a + b + _bdot(a, b), n, p4)
    only32 = jnp.logical_and(same32, jnp.logical_not(same16))
    off1 = _each(lambda low: jnp.where(only32, low, 0.0), lows)
    c1 = _each(lambda a, o: o + _bdot(a, o), n, off1)
    n = _each(lambda a, cc: a - (cc + _bdot(cc, a)), n, c1)
    off2 = _each(lambda low: jnp.where(same32, 0.0, low), lows)
    c2 = _each(lambda a, o: o + _bdot(a, o), n, off2)
    n = _each(lambda a, cc: a - (cc + _bdot(cc, a)), n, c2)
    return n


def _gdn_kernel(pq_ref, pk_ref, pv_ref, pz_ref, cwq_ref, cwk_ref, cwv_ref, gcol_ref, grow_ref,
                nw_ref, o_ref, hq_ref, hk_ref, hv_ref, state_ref, *, hb):
    c_idx = pl.program_id(2)
    C = CHUNK
    w_blk = hb * HEAD_W

    @pl.when(c_idx == 0)
    def _():
        state_ref[...] = jnp.zeros_like(state_ref)
        zero = jnp.zeros((C, w_blk), BF16)
        hq_ref[0:C, :] = zero
        hk_ref[0:C, :] = zero
        hv_ref[0:C, :] = zero

    sr = lax.broadcasted_iota(jnp.int32, ((CONV_TAPS - 1) * C, 2 * C), 0)
    sc = lax.broadcasted_iota(jnp.int32, ((CONV_TAPS - 1) * C, 2 * C), 1)
    tap = sr // C
    shift_mat = jnp.where(sc == sr - tap * (C - 1) + (C - (CONV_TAPS - 1)), 1.0, 0.0).astype(BF16)

    def conv_silu(p_ref, h_ref, cw_ref):
        cur = p_ref[...]
        h_ref[C:2 * C, :] = cur
        shifted = jnp.dot(shift_mat, h_ref[...], preferred_element_type=F32)
        w = cw_ref[...]
        y = shifted[0:C] * w[0:1, :]
        for j in range(1, CONV_TAPS - 1):
            y = y + shifted[j * C:(j + 1) * C] * w[j:j + 1, :]
        y = y + cur.astype(F32) * w[CONV_TAPS - 1:CONV_TAPS, :]
        h_ref[0:C, :] = cur
        return y * jax.nn.sigmoid(y)

    yq = conv_silu(pq_ref, hq_ref, cwq_ref)
    yk = conv_silu(pk_ref, hk_ref, cwk_ref)
    yv = conv_silu(pv_ref, hv_ref, cwv_ref)

    row = lax.broadcasted_iota(jnp.int32, (C, C), 0)
    col = lax.broadcasted_iota(jnp.int32, (C, C), 1)
    causal = row >= col
    strict = row > col
    same16 = (row // 16) == (col // 16)
    same32 = (row // 32) == (col // 32)
    tril = jnp.where(causal, 1.0, 0.0).astype(BF16)
    triu = jnp.where(row <= col, 1.0, 0.0).astype(BF16)

    gcol = gcol_ref[...]
    grow = grow_ref[...]
    gh, gm, gl = _split3(gcol[:, hb:])
    gc_col = (jnp.dot(tril, gh, preferred_element_type=F32)
              + jnp.dot(tril, gm, preferred_element_type=F32)
              + jnp.dot(tril, gl, preferred_element_type=F32))
    rh, rm, rl = _split3(grow[hb:, :])
    gc_row = (jnp.dot(rh, triu, preferred_element_type=F32)
              + jnp.dot(rm, triu, preferred_element_type=F32)
              + jnp.dot(rl, triu, preferred_element_type=F32))

    nw = nw_ref[...]
    q_scale = HEAD_W ** -0.5

    def l2n(y, scale):
        return y * (lax.rsqrt(jnp.sum(y * y, axis=-1, keepdims=True) + GDN_EPS) * scale)

    def heads_stagewise(heads):
        sl = [slice(h * HEAD_W, (h + 1) * HEAD_W) for h in heads]
        q = [l2n(yq[:, s], q_scale) for s in sl]
        k = [l2n(yk[:, s], 1.0) for s in sl]
        v = [yv[:, s] for s in sl]
        beta = [gcol[:, h:h + 1] for h in heads]
        g_c = [gc_col[:, h:h + 1] for h in heads]
        g_r = [gc_row[h:h + 1, :] for h in heads]
        g_last = [r[:, C - 1:C] for r in g_r]
        decay = _each(lambda gc, gr: jnp.exp(jnp.where(causal, gc - gr, -jnp.inf)), g_c, g_r)
        eg = _each(jnp.exp, g_c)
        k_beta = _each(lambda a, b: a * b, k, beta)
        kq = _each(lambda kb, qq, kk: _bdot_nt(jnp.concatenate([kb, qq], axis=0), kk), k_beta, q, k)
        low = _each(lambda a, dc: jnp.where(strict, a[:C] * dc, 0.0), kq, decay)
        attn = _each(lambda a, dc: a[C:] * dc, kq, decay)
        n_inv = _unit_lower_inverse_minus_identity(low, same16, same32)
        rhs = _each(lambda vv, b, kb, e: jnp.concatenate([vv * b, kb * e], axis=1), v, beta, k_beta, eg)
        sol = _each(lambda r, n: r + _bdot(n, r), rhs, n_inv)
        state = [state_ref[h] for h in heads]
        ws = _each(lambda s, qq, e, st: _bdot(jnp.concatenate([s[:, HEAD_W:], qq * e], axis=0), st),
                   sol, q, eg, state)
        v_new = _each(lambda s, a: s[:, :HEAD_W] - a[:C], sol, ws)
        o = _each(lambda a, at, vn: a[C:] + _bdot(at, vn), ws, attn, v_new)
        k_dec = _each(lambda kk, gl_, gc: kk * jnp.exp(gl_ - gc), k, g_last, g_c)
        new_state = _each(lambda st, gl_, kd, vn: st * jnp.exp(gl_) + _bdot_tn(kd, vn),
                          state, g_last, k_dec, v_new)
        for h, st in zip(heads, new_state):
            state_ref[h] = st
        o = _each(lambda a: a * lax.rsqrt(jnp.mean(a * a, axis=-1, keepdims=True) + GDN_EPS) * nw, o)
        for s, a in zip(sl, o):
            zh = pz_ref[:, s].astype(F32)
            o_ref[:, s] = (a * (zh * jax.nn.sigmoid(zh))).astype(o_ref.dtype)

    heads_stagewise(list(range(hb)))


def _gdn_core(p3, conv_w, gcol, grow, norm_w, hb):
    b, s, w4 = p3.shape
    assert p3.dtype == BF16
    assert s % CHUNK == 0
    v_w = w4 // 4
    heads = v_w // HEAD_W
    groups = heads // hb
    wb = hb * HEAD_W

    def pspec(part):
        return pl.BlockSpec((None, CHUNK, wb), lambda bi, gi, ci, part=part: (bi, ci, part * groups + gi))

    def cspec(part):
        return pl.BlockSpec((CONV_TAPS, wb), lambda bi, gi, ci, part=part: (0, part * groups + gi))

    halo = pltpu.VMEM((2 * CHUNK, wb), BF16)
    return pl.pallas_call(
        functools.partial(_gdn_kernel, hb=hb),
        grid=(b, groups, s // CHUNK),
        in_specs=[pspec(0), pspec(1), pspec(2), pspec(3), cspec(0), cspec(1), cspec(2),
                  pl.BlockSpec((None, None, CHUNK, 2 * hb), lambda bi, gi, ci: (bi, gi, ci, 0)),
                  pl.BlockSpec((None, None, None, 2 * hb, CHUNK), lambda bi, gi, ci: (bi, gi, ci, 0, 0)),
                  pl.BlockSpec((1, HEAD_W), lambda bi, gi, ci: (0, 0))],
        out_specs=pl.BlockSpec((None, CHUNK, wb), lambda bi, gi, ci: (bi, ci, gi)),
        out_shape=jax.ShapeDtypeStruct((b, s, v_w), BF16),
        scratch_shapes=[halo, halo, halo, pltpu.VMEM((hb, HEAD_W, HEAD_W), F32)],
        compiler_params=_cparams(("parallel", "parallel", "arbitrary")),
        name="gdn_core",
    )(p3, p3, p3, p3, conv_w, conv_w, conv_w, gcol, grow, norm_w)


def _layer_norm_rows(y, g, b):
    mu = jnp.mean(y, axis=-1, keepdims=True)
    yc = y - mu
    var = jnp.mean(yc * yc, axis=-1, keepdims=True)
    return yc * lax.rsqrt(var + LN_EPS) * g + b


def _row_chunks(rows):
    n = LN_ROW_CHUNKS if rows % (LN_ROW_CHUNKS * V7X_SUBLANES) == 0 else 1
    step = rows // n
    return [slice(k * step, (k + 1) * step) for k in range(n)]


def _matmul_then_ln(chunks, matmul_rows, ln_store_rows):
    pending = None
    for rs in chunks:
        h = matmul_rows(rs)
        if pending is not None:
            ln_store_rows(*pending)
        pending = (rs, h)
    ln_store_rows(*pending)


def _proj_ln_kernel(a_ref, w_ref, x_ref, g_ref, b_ref, o_ref, *, alpha):
    def matmul_rows(rs):
        return jnp.dot(a_ref[rs, :].astype(BF16), w_ref[...], preferred_element_type=F32)

    def ln_store_rows(rs, h):
        o_ref[rs, :] = _layer_norm_rows(alpha * x_ref[rs, :] + h, g_ref[...], b_ref[...])

    _matmul_then_ln(_row_chunks(a_ref.shape[0]), matmul_rows, ln_store_rows)


def _proj_ln(a, w, layer, x2, g, b, alpha, tm):
    t, k = a.shape
    d = w.shape[2]
    tm = min(tm, t)
    vec = pl.BlockSpec((1, d), lambda i: (0, 0))
    return pl.pallas_call(
        functools.partial(_proj_ln_kernel, alpha=alpha),
        grid=(t // tm,),
        in_specs=[pl.BlockSpec((tm, k), lambda i: (i, 0)),
                  pl.BlockSpec((None, k, d), lambda i: (layer, 0, 0)),
                  pl.BlockSpec((tm, d), lambda i: (i, 0)), vec, vec],
        out_specs=pl.BlockSpec((tm, d), lambda i: (i, 0)),
        out_shape=jax.ShapeDtypeStruct((t, d), F32),
        compiler_params=_cparams(("parallel",)),
        name="proj_ln",
    )(a, w, x2, g, b)


def _mlp_ln_kernel(x_ref, wu_ref, wd_ref, g_ref, b_ref, o_ref, xb_ref, acc_ref, *, alpha):
    f = pl.program_id(1)

    @pl.when(f == 0)
    def _():
        xb_ref[...] = x_ref[...].astype(BF16)
        acc_ref[...] = jnp.zeros_like(acc_ref)

    h = jnp.dot(xb_ref[...], wu_ref[...], preferred_element_type=F32)
    h = jnp.maximum(h, 0.0)
    h = h * h
    acc_ref[...] += jnp.dot(h.astype(BF16), wd_ref[...], preferred_element_type=F32)

    @pl.when(f == pl.num_programs(1) - 1)
    def _():
        o_ref[...] = _layer_norm_rows(alpha * x_ref[...] + acc_ref[...], g_ref[...], b_ref[...])


def _mlp_ln(x2, w_up, w_down, layer, g, b, alpha, tm, tf):
    t, d = x2.shape
    ff = w_up.shape[2]
    tm = min(tm, t)
    tf = min(tf, ff)
    vec = pl.BlockSpec((1, d), lambda i, f: (0, 0))
    return pl.pallas_call(
        functools.partial(_mlp_ln_kernel, alpha=alpha),
        grid=(t // tm, ff // tf),
        in_specs=[pl.BlockSpec((tm, d), lambda i, f: (i, 0)),
                  pl.BlockSpec((None, d, tf), lambda i, f: (layer, 0, f)),
                  pl.BlockSpec((None, tf, d), lambda i, f: (layer, f, 0)), vec, vec],
        out_specs=pl.BlockSpec((tm, d), lambda i, f: (i, 0)),
        out_shape=jax.ShapeDtypeStruct((t, d), F32),
        scratch_shapes=[pltpu.VMEM((tm, d), BF16), pltpu.VMEM((tm, d), F32)],
        compiler_params=_cparams(("parallel", "arbitrary")),
        name="mlp_ln",
    )(x2, w_up, w_down, g, b)


def _diff_attn_kernel(q_ref, k_ref, v_ref, lam_ref, sw_ref, o_ref, qs_ref, s_ref, acc_ref,
                      *, tq, tk, nh, lambda_init):
    i = pl.program_id(2)
    heads = list(range(nh))
    hs = [slice(h * HEAD_W, (h + 1) * HEAD_W) for h in heads]
    half = HEAD_W // 2
    lane = lax.broadcasted_iota(jnp.int32, (tq, HEAD_W), 1)
    for h in heads:
        q = q_ref[:, hs[h]]
        zero = jnp.zeros_like(q)
        qs_ref[h, 0:tq, :] = jnp.where(lane < half, q, zero)
        qs_ref[h, tq:2 * tq, :] = jnp.where(lane >= half, q, zero)
    acc_ref[...] = jnp.zeros_like(acc_ref)
    cw = min(V7X_MXU_COLS, tq)
    assert tq % cw == 0
    ones_rows = jnp.ones((ONES_ROWS, tk), BF16)

    def produce(slot, j, diag=None):
        rows = pl.ds(pl.multiple_of(j * tk, tk), tk)
        first = 0 if diag is None else diag * tk
        for h in heads:
            kb = k_ref[rows, hs[h]]
            for lo, hi in ([(0, 2 * tq)] if first == 0 else [(first, tq), (tq + first, 2 * tq)]):
                s_ref[h, slot, :, lo:hi] = lax.dot_general(kb, qs_ref[h, lo:hi, :], (((1,), (1,)), ((), ())),
                                                           preferred_element_type=F32)

    def consume(slot, j, m_prev, diag):
        rows = pl.ds(pl.multiple_of(j * tk, tk), tk)
        lhs = [jnp.concatenate([v_ref[rows, hs[h]].T, ones_rows], axis=0) for h in heads]
        m_parts = [[] for _ in heads]
        for c in range(2 * tq // cw):
            cs = slice(c * cw, (c + 1) * cw)
            r_min = (c * cw) % tq
            if diag is not None and diag * tk > r_min + cw - 1:
                for h in heads:
                    m_parts[h].append(m_prev[h][:, cs])
                continue
            nk = tk if diag is None else min(tk, r_min + cw - diag * tk)
            for h in heads:
                m_prev_c = m_prev[h][:, cs]
                sc = s_ref[h, slot, 0:nk, cs]
                if diag is not None and diag * tk + nk - 1 > r_min:
                    key = lax.broadcasted_iota(jnp.int32, (nk, cw), 0) + diag * tk
                    qry = lax.broadcasted_iota(jnp.int32, (nk, cw), 1) + r_min
                    sc = jnp.where(key <= qry, sc, -jnp.inf)
                m_new_c = jnp.maximum(m_prev_c, jnp.max(sc, axis=0, keepdims=True))
                alpha_c = jnp.exp2(m_prev_c - m_new_c)
                p_c = jnp.exp2(sc - m_new_c).astype(BF16)
                pv_c = jnp.dot(lhs[h][:, 0:nk], p_c, preferred_element_type=F32)
                acc_ref[h, :, cs] = alpha_c * acc_ref[h, :, cs] + pv_c
                m_parts[h].append(m_new_c)
        return tuple(jnp.concatenate(parts, axis=1) for parts in m_parts)

    assert tq == 2 * tk

    def pair(jj, m):
        produce(1, 2 * jj + 1)
        m = consume(0, 2 * jj, m, None)
        produce(0, 2 * jj + 2)
        return consume(1, 2 * jj + 1, m, None)

    produce(0, 0)
    m = lax.fori_loop(0, i, pair, tuple(jnp.full((1, 2 * tq), -jnp.inf, F32) for _ in heads))
    produce(1, 2 * i + 1, 1)
    m = consume(0, 2 * i, m, 0)
    m = consume(1, 2 * i + 1, m, 1)

    lp = lam_ref[...]
    lam = (jnp.exp(jnp.sum(lp[0:1, :] * lp[1:2, :], axis=-1, keepdims=True))
           - jnp.exp(jnp.sum(lp[2:3, :] * lp[3:4, :], axis=-1, keepdims=True)) + lambda_init)
    o_all = [acc_ref[h, 0:HEAD_W, :] * (1.0 / acc_ref[h, HEAD_W:HEAD_W + 1, :]) for h in heads]
    o = [(a[:, 0:tq] - lam * a[:, tq:2 * tq]).T for a in o_all]
    o = [a * lax.rsqrt(jnp.mean(a * a, axis=-1, keepdims=True) + SUBLN_EPS) * sw_ref[...] for a in o]
    for h in heads:
        o_ref[:, hs[h]] = (o[h] * (1.0 - lambda_init)).astype(o_ref.dtype)


def _diff_attention(q3, kv3, lam_params, subln_w, lambda_init, tq, tk, nh):
    b, s, wq = q3.shape
    heads = wq // HEAD_W
    tq = min(tq, s)
    tk = min(tk, tq)
    nh = min(nh, heads)
    groups = heads // nh
    wb = nh * HEAD_W
    return pl.pallas_call(
        functools.partial(_diff_attn_kernel, tq=tq, tk=tk, nh=nh, lambda_init=lambda_init),
        grid=(b, groups, s // tq),
        in_specs=[pl.BlockSpec((None, tq, wb), lambda bi, g, i: (bi, i, g)),
                  pl.BlockSpec((None, s, wb), lambda bi, g, i: (bi, 0, g)),
                  pl.BlockSpec((None, s, wb), lambda bi, g, i, groups=groups: (bi, 0, groups + g)),
                  pl.BlockSpec(lam_params.shape, lambda bi, g, i: (0, 0)),
                  pl.BlockSpec((1, HEAD_W), lambda bi, g, i: (0, 0))],
        out_specs=pl.BlockSpec((None, tq, wb), lambda bi, g, i: (bi, i, g)),
        out_shape=jax.ShapeDtypeStruct((b, s, wq), BF16),
        scratch_shapes=[pltpu.VMEM((nh, 2 * tq, HEAD_W), BF16),
                        pltpu.VMEM((nh, 2, tk, 2 * tq), F32),
                        pltpu.VMEM((nh, HEAD_W + ONES_ROWS, 2 * tq), F32)],
        compiler_params=_cparams(("parallel", "parallel", "arbitrary")),
        name="diff_attn",
    )(q3, kv3, kv3, lam_params, subln_w)


def _gdn_layer(x2, bsz, seq, layer, w_in_f32, w_in_b16, conv_w, a_log, dt_bias, norm_w, w_out_b16, ln_g, ln_b,
               alpha, hb, tiles):
    t, d = x2.shape
    heads = a_log.shape[0]
    v_w = heads * HEAD_W
    groups = heads // hb
    main = 4 * v_w
    p = _matmul(x2, w_in_b16, layer, BF16, tiles["tm"], tiles["tn"], name="gdn_in_proj", n_cols=main)
    wb = w_in_f32[layer, :, main:main + heads].reshape(d, groups, hb)
    wa = w_in_f32[layer, :, main + heads:main + 2 * heads].reshape(d, groups, hb)
    w_gate = jnp.concatenate([wb, wa], axis=2).reshape(d, 2 * heads)
    w_gate = jnp.pad(w_gate, ((0, 0), (0, GATE_W - 2 * heads)))

    def per_col(vals, fill):
        zeros = jnp.full((groups, hb), fill, F32)
        cols = jnp.concatenate([zeros, vals.astype(F32).reshape(groups, hb)], axis=1).reshape(1, 2 * heads)
        return jnp.pad(cols, ((0, 0), (0, GATE_W - 2 * heads)), constant_values=fill)

    is_a = per_col(jnp.ones((heads,), F32), 0.0)
    neg_a = per_col(-jnp.exp(a_log.astype(F32)), 0.0)
    dtb = per_col(dt_bias, 0.0)
    gates = _gates(x2, w_gate, is_a, neg_a, dtb, tiles["tm_gate"])[:, :2 * heads]
    gcol = gates.reshape(bsz, seq, groups, 2 * hb).transpose(0, 2, 1, 3)
    grow = gates.reshape(bsz, seq // CHUNK, CHUNK, groups, 2 * hb).transpose(0, 3, 1, 4, 2)
    o = _gdn_core(p.reshape(bsz, seq, main), conv_w, gcol, grow, norm_w.reshape(1, HEAD_W), hb)
    return _proj_ln(o.reshape(t, v_w), w_out_b16, layer, x2, ln_g.reshape(1, d), ln_b.reshape(1, d),
                    alpha, tiles["tm_proj"])


def _tiles():
    return dict(tm=1024, tn=2048, tm_gate=512, tm_proj=512, tm_mlp=512, tf_mlp=1024, tq=1024, tk=512, nh=2, hb=16)


def _forward(x, gdn_w_in, gdn_conv_w, gdn_a_log, gdn_dt_bias, gdn_norm_w, gdn_w_out, diff_w_q, diff_lambda,
             diff_subln_w, diff_w_o, shared_w_kv, mlp_w_up, mlp_w_down, ln_g, ln_b, tiles):
    bsz, seq, d = x.shape
    t = bsz * seq
    depth = mlp_w_up.shape[0]
    n_a = gdn_w_in.shape[0]
    alpha = (2 * depth) ** 0.25
    x2 = x.reshape(t, d).astype(F32)
    w_in_b16 = gdn_w_in.astype(BF16)
    w_out_b16 = gdn_w_out.astype(BF16)
    w_q_b16 = diff_w_q.astype(BF16)
    w_o_b16 = diff_w_o.astype(BF16)
    w_kv_b16 = shared_w_kv.astype(BF16)[None]
    w_up_b16 = mlp_w_up.astype(BF16)
    w_down_b16 = mlp_w_down.astype(BF16)
    kv3 = None
    for l in range(depth):
        if l < n_a:
            x2 = _gdn_layer(x2, bsz, seq, l, gdn_w_in, w_in_b16, gdn_conv_w[l], gdn_a_log[l], gdn_dt_bias[l],
                            gdn_norm_w[l], w_out_b16, ln_g[l, 0], ln_b[l, 0], alpha,
                            min(tiles["hb"], gdn_a_log.shape[1]), tiles)
        else:
            j = l - n_a
            lambda_init = 0.8 - 0.6 * math.exp(-0.3 * l)
            dk = diff_lambda.shape[-1]
            q = _matmul(x2, w_q_b16, j, BF16, tiles["tm"], tiles["tn"],
                        scale=dk ** -0.5 * math.log2(math.e), name="diff_q_proj")
            o = _diff_attention(q.reshape(bsz, seq, -1), kv3, diff_lambda[j].astype(F32),
                                diff_subln_w[j].reshape(1, HEAD_W).astype(F32), lambda_init, tiles["tq"],
                                tiles["tk"], tiles["nh"])
            x2 = _proj_ln(o.reshape(t, -1), w_o_b16, j, x2, ln_g[l, 0].reshape(1, d),
                          ln_b[l, 0].reshape(1, d), alpha, tiles["tm_proj"])
        x2 = _mlp_ln(x2, w_up_b16, w_down_b16, l, ln_g[l, 1].reshape(1, d),
                     ln_b[l, 1].reshape(1, d), alpha, tiles["tm_mlp"], tiles["tf_mlp"])
        if l == n_a - 1:
            kv = _matmul(x2, w_kv_b16, 0, BF16, tiles["tm"], tiles["tn"], name="shared_kv_proj")
            kv3 = kv.reshape(bsz, seq, -1)
    return x2.reshape(bsz, seq, d).astype(x.dtype)


def kernel(x, gdn_w_in, gdn_conv_w, gdn_a_log, gdn_dt_bias, gdn_norm_w, gdn_w_out, diff_w_q, diff_lambda,
           diff_subln_w, diff_w_o, shared_w_kv, mlp_w_up, mlp_w_down, ln_g, ln_b):
    return _forward(x, gdn_w_in, gdn_conv_w, gdn_a_log, gdn_dt_bias, gdn_norm_w, gdn_w_out, diff_w_q,
                    diff_lambda, diff_subln_w, diff_w_o, shared_w_kv, mlp_w_up, mlp_w_down, ln_g, ln_b,
                    _tiles())
```

```python
import functools
import math

import jax
import jax.numpy as jnp
from jax import lax
from jax.experimental import pallas as pl
from jax.experimental.pallas import tpu as pltpu

F32 = jnp.float32
BF16 = jnp.bfloat16

HEAD_W = 128
CHUNK = 64
CONV_TAPS = 4
GDN_EPS = 1e-6
SUBLN_EPS = 1e-5
LN_EPS = 1e-5
GATE_W = 128
LN_ROW_CHUNKS = 4

V7X_SUBLANES = 8
V7X_MXU_COLS = 256
ONES_ROWS = 16
V7X_VMEM_LIMIT_BYTES = 56 * 1024 * 1024


def _cparams(sem):
    return pltpu.CompilerParams(dimension_semantics=sem, vmem_limit_bytes=V7X_VMEM_LIMIT_BYTES)


def _bdot(a, b):
    return jnp.dot(a.astype(BF16), b.astype(BF16), preferred_element_type=F32)


def _bdot_nt(a, b):
    return lax.dot_general(a.astype(BF16), b.astype(BF16), (((1,), (1,)), ((), ())),
                           preferred_element_type=F32)


def _bdot_tn(a, b):
    return lax.dot_general(a.astype(BF16), b.astype(BF16), (((0,), (0,)), ((), ())),
                           preferred_element_type=F32)


def _split3(x):
    h = x.astype(BF16)
    r = x - h.astype(F32)
    m = r.astype(BF16)
    l = (r - m.astype(F32)).astype(BF16)
    return h, m, l


def _mm_kernel(x_ref, w_ref, o_ref, *, scale):
    acc = jnp.dot(x_ref[...].astype(BF16), w_ref[...], preferred_element_type=F32)
    if scale != 1.0:
        acc = acc * scale
    o_ref[...] = acc.astype(o_ref.dtype)


def _matmul(x, w, layer, out_dtype, tm, tn, scale=1.0, name="matmul"):
    m, k = x.shape
    n = w.shape[2]
    tm = min(tm, m)
    tn = min(tn, n)
    assert m % tm == 0 and n % tn == 0
    return pl.pallas_call(
        functools.partial(_mm_kernel, scale=scale),
        grid=(m // tm, n // tn),
        in_specs=[pl.BlockSpec((tm, k), lambda i, j: (i, 0)),
                  pl.BlockSpec((None, k, tn), lambda i, j: (layer, 0, j))],
        out_specs=pl.BlockSpec((tm, tn), lambda i, j: (i, j)),
        out_shape=jax.ShapeDtypeStruct((m, n), out_dtype),
        compiler_params=_cparams(("parallel", "parallel")),
        name=name,
    )(x, w)


def _gate_kernel(x_ref, w_ref, isa_ref, nega_ref, dtb_ref, o_ref):
    x = x_ref[...]
    w = w_ref[...]
    xh = x.astype(BF16)
    xl = (x - xh.astype(F32)).astype(BF16)
    wh = w.astype(BF16)
    wl = (w - wh.astype(F32)).astype(BF16)
    acc = (jnp.dot(xh, wh, preferred_element_type=F32)
           + jnp.dot(xl, wh, preferred_element_type=F32)
           + jnp.dot(xh, wl, preferred_element_type=F32))
    beta = jax.nn.sigmoid(acc)
    s = acc + dtb_ref[...]
    softplus = jnp.maximum(s, 0.0) + jnp.log1p(jnp.exp(-jnp.abs(s)))
    g = nega_ref[...] * softplus
    o_ref[...] = jnp.where(isa_ref[...] > 0.5, g, beta)


def _gates(x2, w_gate, is_a, neg_a, dtb, tm):
    t, d = x2.shape
    tm = min(tm, t)
    vec = pl.BlockSpec((1, GATE_W), lambda i: (0, 0))
    return pl.pallas_call(
        _gate_kernel,
        grid=(t // tm,),
        in_specs=[pl.BlockSpec((tm, d), lambda i: (i, 0)),
                  pl.BlockSpec((d, GATE_W), lambda i: (0, 0)), vec, vec, vec],
        out_specs=pl.BlockSpec((tm, GATE_W), lambda i: (i, 0)),
        out_shape=jax.ShapeDtypeStruct((t, GATE_W), F32),
        compiler_params=_cparams(("parallel",)),
        name="gdn_gates",
    )(x2, w_gate, is_a, neg_a, dtb)


def _each(fn, *lists):
    return [fn(*args) for args in zip(*lists)]


def _unit_lower_inverse_minus_identity(lows, same16, same32):
    c = lows[0].shape[0]
    d = _each(lambda low: jnp.where(same16, low, 0.0), lows)
    x = _each(lambda a: -a, d)
    p = _each(lambda a: _bdot(a, a), d)
    xp = _each(lambda a, b: _bdot(jnp.concatenate([a, b], axis=0), b), x, p)
    n = _each(lambda a, b, ab: a + b + ab[:c], x, p, xp)
    p2 = _each(lambda ab: ab[c:], xp)
    np2 = _each(lambda a, b: _bdot(jnp.concatenate([a, b], axis=0), b), n, p2)
    n = _each(lambda a, b, ab: a + b + ab[:c], n, p2, np2)
    p4 = _each(lambda ab: ab[c:], np2)
    n = _each(lambda a, b: a + b + _bdot(a, b), n, p4)
    only32 = jnp.logical_and(same32, jnp.logical_not(same16))
    off1 = _each(lambda low: jnp.where(only32, low, 0.0), lows)
    c1 = _each(lambda a, o: o + _bdot(a, o), n, off1)
    n = _each(lambda a, cc: a - (cc + _bdot(cc, a)), n, c1)
    off2 = _each(lambda low: jnp.where(same32, 0.0, low), lows)
    c2 = _each(lambda a, o: o + _bdot(a, o), n, off2)
    n = _each(lambda a, cc: a - (cc + _bdot(cc, a)), n, c2)
    return n


def _gdn_kernel(pq_ref, pk_ref, pv_ref, pz_ref, cwq_ref, cwk_ref, cwv_ref, gcol_ref, grow_ref,
                nw_ref, o_ref, hq_ref, hk_ref, hv_ref, state_ref, *, hb):
    c_idx = pl.program_id(2)
    C = CHUNK
    w_blk = hb * HEAD_W

    @pl.when(c_idx == 0)
    def _():
        state_ref[...] = jnp.zeros_like(state_ref)
        zero = jnp.zeros((C, w_blk), BF16)
        hq_ref[0:C, :] = zero
        hk_ref[0:C, :] = zero
        hv_ref[0:C, :] = zero

    sr = lax.broadcasted_iota(jnp.int32, ((CONV_TAPS - 1) * C, 2 * C), 0)
    sc = lax.broadcasted_iota(jnp.int32, ((CONV_TAPS - 1) * C, 2 * C), 1)
    tap = sr // C
    shift_mat = jnp.where(sc == sr - tap * (C - 1) + (C - (CONV_TAPS - 1)), 1.0, 0.0).astype(BF16)

    def conv_silu(p_ref, h_ref, cw_ref):
        cur = p_ref[...]
        h_ref[C:2 * C, :] = cur
        shifted = jnp.dot(shift_mat, h_ref[...], preferred_element_type=F32)
        w = cw_ref[...]
        y = shifted[0:C] * w[0:1, :]
        for j in range(1, CONV_TAPS - 1):
            y = y + shifted[j * C:(j + 1) * C] * w[j:j + 1, :]
        y = y + cur.astype(F32) * w[CONV_TAPS - 1:CONV_TAPS, :]
        h_ref[0:C, :] = cur
        return y * jax.nn.sigmoid(y)

    yq = conv_silu(pq_ref, hq_ref, cwq_ref)
    yk = conv_silu(pk_ref, hk_ref, cwk_ref)
    yv = conv_silu(pv_ref, hv_ref, cwv_ref)

    row = lax.broadcasted_iota(jnp.int32, (C, C), 0)
    col = lax.broadcasted_iota(jnp.int32, (C, C), 1)
    causal = row >= col
    strict = row > col
    same16 = (row // 16) == (col // 16)
    same32 = (row // 32) == (col // 32)
    tril = jnp.where(causal, 1.0, 0.0).astype(BF16)
    triu = jnp.where(row <= col, 1.0, 0.0).astype(BF16)

    gcol = gcol_ref[...]
    grow = grow_ref[...]
    gh, gm, gl = _split3(gcol[:, hb:])
    gc_col = (jnp.dot(tril, gh, preferred_element_type=F32)
              + jnp.dot(tril, gm, preferred_element_type=F32)
              + jnp.dot(tril, gl, preferred_element_type=F32))
    rh, rm, rl = _split3(grow[hb:, :])
    gc_row = (jnp.dot(rh, triu, preferred_element_type=F32)
              + jnp.dot(rm, triu, preferred_element_type=F32)
              + jnp.dot(rl, triu, preferred_element_type=F32))

    nw = nw_ref[...]
    q_scale = HEAD_W ** -0.5

    def l2n(y, scale):
        return y * (lax.rsqrt(jnp.sum(y * y, axis=-1, keepdims=True) + GDN_EPS) * scale)

    def heads_stagewise(heads):
        sl = [slice(h * HEAD_W, (h + 1) * HEAD_W) for h in heads]
        q = [l2n(yq[:, s], q_scale) for s in sl]
        k = [l2n(yk[:, s], 1.0) for s in sl]
        v = [yv[:, s] for s in sl]
        beta = [gcol[:, h:h + 1] for h in heads]
        g_c = [gc_col[:, h:h + 1] for h in heads]
        g_r = [gc_row[h:h + 1, :] for h in heads]
        g_last = [r[:, C - 1:C] for r in g_r]
        decay = _each(lambda gc, gr: jnp.exp(jnp.where(causal, gc - gr, -jnp.inf)), g_c, g_r)
        eg = _each(jnp.exp, g_c)
        k_beta = _each(lambda a, b: a * b, k, beta)
        kq = _each(lambda kb, qq, kk: _bdot_nt(jnp.concatenate([kb, qq], axis=0), kk), k_beta, q, k)
        low = _each(lambda a, dc: jnp.where(strict, a[:C] * dc, 0.0), kq, decay)
        attn = _each(lambda a, dc: a[C:] * dc, kq, decay)
        n_inv = _unit_lower_inverse_minus_identity(low, same16, same32)
        rhs = _each(lambda vv, b, kb, e: jnp.concatenate([vv * b, kb * e], axis=1), v, beta, k_beta, eg)
        sol = _each(lambda r, n: r + _bdot(n, r), rhs, n_inv)
        state = [state_ref[h] for h in heads]
        ws = _each(lambda s, qq, e, st: _bdot(jnp.concatenate([s[:, HEAD_W:], qq * e], axis=0), st),
                   sol, q, eg, state)
        v_new = _each(lambda s, a: s[:, :HEAD_W] - a[:C], sol, ws)
        o = _each(lambda a, at, vn: a[C:] + _bdot(at, vn), ws, attn, v_new)
        k_dec = _each(lambda kk, gl_, gc: kk * jnp.exp(gl_ - gc), k, g_last, g_c)
        new_state = _each(lambda st, gl_, kd, vn: st * jnp.exp(gl_) + _bdot_tn(kd, vn),
                          state, g_last, k_dec, v_new)
        for h, st in zip(heads, new_state):
            state_ref[h] = st
        o = _each(lambda a: a * lax.rsqrt(jnp.mean(a * a, axis=-1, keepdims=True) + GDN_EPS) * nw, o)
        for s, a in zip(sl, o):
            zh = pz_ref[:, s].astype(F32)
            o_ref[:, s] = (a * (zh * jax.nn.sigmoid(zh))).astype(o_ref.dtype)

    heads_stagewise(list(range(hb)))


def _gdn_core(p3, conv_w, gcol, grow, norm_w, hb):
    b, s, w4 = p3.shape
    assert p3.dtype == BF16
    assert s % CHUNK == 0
    v_w = w4 // 4
    heads = v_w // HEAD_W
    groups = heads // hb
    wb = hb * HEAD_W

    def pspec(part):
        return pl.BlockSpec((None, CHUNK, wb), lambda bi, gi, ci, part=part: (bi, ci, part * groups + gi))

    def cspec(part):
        return pl.BlockSpec((CONV_TAPS, wb), lambda bi, gi, ci, part=part: (0, part * groups + gi))

    halo = pltpu.VMEM((2 * CHUNK, wb), BF16)
    return pl.pallas_call(
        functools.partial(_gdn_kernel, hb=hb),
        grid=(b, groups, s // CHUNK),
        in_specs=[pspec(0), pspec(1), pspec(2), pspec(3), cspec(0), cspec(1), cspec(2),
                  pl.BlockSpec((None, None, CHUNK, 2 * hb), lambda bi, gi, ci: (bi, gi, ci, 0)),
                  pl.BlockSpec((None, None, None, 2 * hb, CHUNK), lambda bi, gi, ci: (bi, gi, ci, 0, 0)),
                  pl.BlockSpec((1, HEAD_W), lambda bi, gi, ci: (0, 0))],
        out_specs=pl.BlockSpec((None, CHUNK, wb), lambda bi, gi, ci: (bi, ci, gi)),
        out_shape=jax.ShapeDtypeStruct((b, s, v_w), BF16),
        scratch_shapes=[halo, halo, halo, pltpu.VMEM((hb, HEAD_W, HEAD_W), F32)],
        compiler_params=_cparams(("parallel", "parallel", "arbitrary")),
        name="gdn_core",
    )(p3, p3, p3, p3, conv_w, conv_w, conv_w, gcol, grow, norm_w)


def _layer_norm_rows(y, g, b):
    mu = jnp.mean(y, axis=-1, keepdims=True)
    yc = y - mu
    var = jnp.mean(yc * yc, axis=-1, keepdims=True)
    return yc * lax.rsqrt(var + LN_EPS) * g + b


def _row_chunks(rows):
    n = LN_ROW_CHUNKS if rows % (LN_ROW_CHUNKS * V7X_SUBLANES) == 0 else 1
    step = rows // n
    return [slice(k * step, (k + 1) * step) for k in range(n)]


def _matmul_then_ln(chunks, matmul_rows, ln_store_rows):
    pending = None
    for rs in chunks:
        h = matmul_rows(rs)
        if pending is not None:
            ln_store_rows(*pending)
        pending = (rs, h)
    ln_store_rows(*pending)


def _proj_ln_kernel(a_ref, w_ref, x_ref, g_ref, b_ref, o_ref, *, alpha):
    def matmul_rows(rs):
        return jnp.dot(a_ref[rs, :].astype(BF16), w_ref[...], preferred_element_type=F32)

    def ln_store_rows(rs, h):
        o_ref[rs, :] = _layer_norm_rows(alpha * x_ref[rs, :] + h, g_ref[...], b_ref[...])

    _matmul_then_ln(_row_chunks(a_ref.shape[0]), matmul_rows, ln_store_rows)


def _proj_ln(a, w, layer, x2, g, b, alpha, tm):
    t, k = a.shape
    d = w.shape[2]
    tm = min(tm, t)
    vec = pl.BlockSpec((1, d), lambda i: (0, 0))
    return pl.pallas_call(
        functools.partial(_proj_ln_kernel, alpha=alpha),
        grid=(t // tm,),
        in_specs=[pl.BlockSpec((tm, k), lambda i: (i, 0)),
                  pl.BlockSpec((None, k, d), lambda i: (layer, 0, 0)),
                  pl.BlockSpec((tm, d), lambda i: (i, 0)), vec, vec],
        out_specs=pl.BlockSpec((tm, d), lambda i: (i, 0)),
        out_shape=jax.ShapeDtypeStruct((t, d), F32),
        compiler_params=_cparams(("parallel",)),
        name="proj_ln",
    )(a, w, x2, g, b)


def _mlp_ln_kernel(x_ref, wu_ref, wd_ref, g_ref, b_ref, o_ref, xb_ref, acc_ref, *, alpha):
    f = pl.program_id(1)

    @pl.when(f == 0)
    def _():
        xb_ref[...] = x_ref[...].astype(BF16)
        acc_ref[...] = jnp.zeros_like(acc_ref)

    h = jnp.dot(xb_ref[...], wu_ref[...], preferred_element_type=F32)
    h = jnp.maximum(h, 0.0)
    h = h * h
    acc_ref[...] += jnp.dot(h.astype(BF16), wd_ref[...], preferred_element_type=F32)

    @pl.when(f == pl.num_programs(1) - 1)
    def _():
        o_ref[...] = _layer_norm_rows(alpha * x_ref[...] + acc_ref[...], g_ref[...], b_ref[...])


def _mlp_ln(x2, w_up, w_down, layer, g, b, alpha, tm, tf):
    t, d = x2.shape
    ff = w_up.shape[2]
    tm = min(tm, t)
    tf = min(tf, ff)
    vec = pl.BlockSpec((1, d), lambda i, f: (0, 0))
    return pl.pallas_call(
        functools.partial(_mlp_ln_kernel, alpha=alpha),
        grid=(t // tm, ff // tf),
        in_specs=[pl.BlockSpec((tm, d), lambda i, f: (i, 0)),
                  pl.BlockSpec((None, d, tf), lambda i, f: (layer, 0, f)),
                  pl.BlockSpec((None, tf, d), lambda i, f: (layer, f, 0)), vec, vec],
        out_specs=pl.BlockSpec((tm, d), lambda i, f: (i, 0)),
        out_shape=jax.ShapeDtypeStruct((t, d), F32),
        scratch_shapes=[pltpu.VMEM((tm, d), BF16), pltpu.VMEM((tm, d), F32)],
        compiler_params=_cparams(("parallel", "arbitrary")),
        name="mlp_ln",
    )(x2, w_up, w_down, g, b)


def _diff_attn_kernel(q_ref, k_ref, v_ref, lam_ref, sw_ref, o_ref, qs_ref, s_ref, acc_ref,
                      *, tq, tk, nh, lambda_init):
    nq = q_ref.shape[0] // tq
    heads = list(range(nh))
    hs = [slice(h * HEAD_W, (h + 1) * HEAD_W) for h in heads]
    half = HEAD_W // 2
    lane = lax.broadcasted_iota(jnp.int32, (tq, HEAD_W), 1)
    cw = min(V7X_MXU_COLS, tq)
    assert tq % cw == 0
    ones_rows = jnp.ones((ONES_ROWS, tk), BF16)

    def stack_queries(i):
        qrows = pl.ds(pl.multiple_of(i * tq, tq), tq)
        for h in heads:
            q = q_ref[qrows, hs[h]]
            zero = jnp.zeros_like(q)
            qs_ref[h, 0:tq, :] = jnp.where(lane < half, q, zero)
            qs_ref[h, tq:2 * tq, :] = jnp.where(lane >= half, q, zero)

    def produce(slot, j, diag=None):
        rows = pl.ds(pl.multiple_of(j * tk, tk), tk)
        first = 0 if diag is None else diag * tk
        for h in heads:
            kb = k_ref[rows, hs[h]]
            for lo, hi in ([(0, 2 * tq)] if first == 0 else [(first, tq), (tq + first, 2 * tq)]):
                s_ref[h, slot, :, lo:hi] = lax.dot_general(kb, qs_ref[h, lo:hi, :], (((1,), (1,)), ((), ())),
                                                           preferred_element_type=F32)

    def consume(slot, j, m_prev, diag):
        rows = pl.ds(pl.multiple_of(j * tk, tk), tk)
        lhs = [jnp.concatenate([v_ref[rows, hs[h]].T, ones_rows], axis=0) for h in heads]
        m_parts = [[] for _ in heads]
        for c in range(2 * tq // cw):
            cs = slice(c * cw, (c + 1) * cw)
            r_min = (c * cw) % tq
            if diag is not None and diag * tk > r_min + cw - 1:
                for h in heads:
                    m_parts[h].append(m_prev[h][:, cs])
                continue
            nk = tk if diag is None else min(tk, r_min + cw - diag * tk)
            for h in heads:
                m_prev_c = m_prev[h][:, cs]
                sc = s_ref[h, slot, 0:nk, cs]
                if diag is not None and diag * tk + nk - 1 > r_min:
                    key = lax.broadcasted_iota(jnp.int32, (nk, cw), 0) + diag * tk
                    qry = lax.broadcasted_iota(jnp.int32, (nk, cw), 1) + r_min
                    sc = jnp.where(key <= qry, sc, -jnp.inf)
                m_new_c = jnp.maximum(m_prev_c, jnp.max(sc, axis=0, keepdims=True))
                alpha_c = jnp.exp2(m_prev_c - m_new_c)
                p_c = jnp.exp2(sc - m_new_c).astype(BF16)
                pv_c = jnp.dot(lhs[h][:, 0:nk], p_c, preferred_element_type=F32)
                acc_ref[h, :, cs] = alpha_c * acc_ref[h, :, cs] + pv_c
                m_parts[h].append(m_new_c)
        return tuple(jnp.concatenate(parts, axis=1) for parts in m_parts)

    assert tq == 2 * tk

    def pair(jj, m):
        produce(1, 2 * jj + 1)
        m = consume(0, 2 * jj, m, None)
        produce(0, 2 * jj + 2)
        return consume(1, 2 * jj + 1, m, None)

    def attend(i):
        m = lax.fori_loop(0, i, pair, tuple(jnp.full((1, 2 * tq), -jnp.inf, F32) for _ in heads))
        produce(1, 2 * i + 1, 1)
        m = consume(0, 2 * i, m, 0)
        consume(1, 2 * i + 1, m, 1)

    lp = lam_ref[...]
    lam = (jnp.exp(jnp.sum(lp[0:1, :] * lp[1:2, :], axis=-1, keepdims=True))
           - jnp.exp(jnp.sum(lp[2:3, :] * lp[3:4, :], axis=-1, keepdims=True)) + lambda_init)

    def finalize(i):
        qrows = pl.ds(pl.multiple_of(i * tq, tq), tq)
        o_all = [acc_ref[h, 0:HEAD_W, :] * (1.0 / acc_ref[h, HEAD_W:HEAD_W + 1, :]) for h in heads]
        o = [(a[:, 0:tq] - lam * a[:, tq:2 * tq]).T for a in o_all]
        o = [a * lax.rsqrt(jnp.mean(a * a, axis=-1, keepdims=True) + SUBLN_EPS) * sw_ref[...] for a in o]
        for h in heads:
            o_ref[qrows, hs[h]] = (o[h] * (1.0 - lambda_init)).astype(o_ref.dtype)

    acc_ref[...] = jnp.ones_like(acc_ref)

    def query_block(i, carry):
        stack_queries(i)
        produce(0, 0)
        finalize(jnp.maximum(i - 1, 0))
        acc_ref[...] = jnp.zeros_like(acc_ref)
        attend(i)
        return carry

    lax.fori_loop(0, nq, query_block, 0)
    finalize(nq - 1)


def _diff_attention(q3, kv3, lam_params, subln_w, lambda_init, tq, tk, nh):
    b, s, wq = q3.shape
    heads = wq // HEAD_W
    tq = min(tq, s)
    tk = min(tk, tq)
    nh = min(nh, heads)
    groups = heads // nh
    wb = nh * HEAD_W
    return pl.pallas_call(
        functools.partial(_diff_attn_kernel, tq=tq, tk=tk, nh=nh, lambda_init=lambda_init),
        grid=(b, groups),
        in_specs=[pl.BlockSpec((None, s, wb), lambda bi, g: (bi, 0, g)),
                  pl.BlockSpec((None, s, wb), lambda bi, g: (bi, 0, g)),
                  pl.BlockSpec((None, s, wb), lambda bi, g, groups=groups: (bi, 0, groups + g)),
                  pl.BlockSpec(lam_params.shape, lambda bi, g: (0, 0)),
                  pl.BlockSpec((1, HEAD_W), lambda bi, g: (0, 0))],
        out_specs=pl.BlockSpec((None, s, wb), lambda bi, g: (bi, 0, g)),
        out_shape=jax.ShapeDtypeStruct((b, s, wq), BF16),
        scratch_shapes=[pltpu.VMEM((nh, 2 * tq, HEAD_W), BF16),
                        pltpu.VMEM((nh, 2, tk, 2 * tq), F32),
                        pltpu.VMEM((nh, HEAD_W + ONES_ROWS, 2 * tq), F32)],
        compiler_params=_cparams(("parallel", "parallel")),
        name="diff_attn",
    )(q3, kv3, kv3, lam_params, subln_w)


def _gdn_layer(x2, bsz, seq, layer, w_in_f32, w_in_b16, conv_w, a_log, dt_bias, norm_w, w_out_b16, ln_g, ln_b,
               alpha, hb, tiles):
    t, d = x2.shape
    heads = a_log.shape[0]
    v_w = heads * HEAD_W
    groups = heads // hb
    main = 4 * v_w
    p = _matmul(x2, w_in_b16, layer, BF16, tiles["tm"], tiles["tn"], name="gdn_in_proj")
    wb = w_in_f32[layer, :, main:main + heads].reshape(d, groups, hb)
    wa = w_in_f32[layer, :, main + heads:main + 2 * heads].reshape(d, groups, hb)
    w_gate = jnp.concatenate([wb, wa], axis=2).reshape(d, 2 * heads)
    w_gate = jnp.pad(w_gate, ((0, 0), (0, GATE_W - 2 * heads)))

    def per_col(vals, fill):
        zeros = jnp.full((groups, hb), fill, F32)
        cols = jnp.concatenate([zeros, vals.astype(F32).reshape(groups, hb)], axis=1).reshape(1, 2 * heads)
        return jnp.pad(cols, ((0, 0), (0, GATE_W - 2 * heads)), constant_values=fill)

    is_a = per_col(jnp.ones((heads,), F32), 0.0)
    neg_a = per_col(-jnp.exp(a_log.astype(F32)), 0.0)
    dtb = per_col(dt_bias, 0.0)
    gates = _gates(x2, w_gate, is_a, neg_a, dtb, tiles["tm_gate"])[:, :2 * heads]
    gcol = gates.reshape(bsz, seq, groups, 2 * hb).transpose(0, 2, 1, 3)
    grow = gates.reshape(bsz, seq // CHUNK, CHUNK, groups, 2 * hb).transpose(0, 3, 1, 4, 2)
    o = _gdn_core(p.reshape(bsz, seq, main), conv_w, gcol, grow, norm_w.reshape(1, HEAD_W), hb)
    return _proj_ln(o.reshape(t, v_w), w_out_b16, layer, x2, ln_g.reshape(1, d), ln_b.reshape(1, d),
                    alpha, tiles["tm_proj"])


def _tiles():
    return dict(tm=1024, tn=2048, tm_gate=512, tm_proj=512, tm_mlp=512, tf_mlp=1024, tq=1024, tk=512, nh=2, hb=16)


def _forward(x, gdn_w_in, gdn_conv_w, gdn_a_log, gdn_dt_bias, gdn_norm_w, gdn_w_out, diff_w_q, diff_lambda,
             diff_subln_w, diff_w_o, shared_w_kv, mlp_w_up, mlp_w_down, ln_g, ln_b, tiles):
    bsz, seq, d = x.shape
    t = bsz * seq
    depth = mlp_w_up.shape[0]
    n_a = gdn_w_in.shape[0]
    alpha = (2 * depth) ** 0.25
    x2 = x.reshape(t, d).astype(F32)
    w_in_b16 = gdn_w_in[:, :, :4 * gdn_a_log.shape[1] * HEAD_W].astype(BF16)
    w_out_b16 = gdn_w_out.astype(BF16)
    w_q_b16 = diff_w_q.astype(BF16)
    w_o_b16 = diff_w_o.astype(BF16)
    w_kv_b16 = shared_w_kv.astype(BF16)[None]
    w_up_b16 = mlp_w_up.astype(BF16)
    w_down_b16 = mlp_w_down.astype(BF16)
    kv3 = None
    for l in range(depth):
        if l < n_a:
            x2 = _gdn_layer(x2, bsz, seq, l, gdn_w_in, w_in_b16, gdn_conv_w[l], gdn_a_log[l], gdn_dt_bias[l],
                            gdn_norm_w[l], w_out_b16, ln_g[l, 0], ln_b[l, 0], alpha,
                            min(tiles["hb"], gdn_a_log.shape[1]), tiles)
        else:
            j = l - n_a
            lambda_init = 0.8 - 0.6 * math.exp(-0.3 * l)
            dk = diff_lambda.shape[-1]
            q = _matmul(x2, w_q_b16, j, BF16, tiles["tm"], tiles["tn"],
                        scale=dk ** -0.5 * math.log2(math.e), name="diff_q_proj")
            o = _diff_attention(q.reshape(bsz, seq, -1), kv3, diff_lambda[j].astype(F32),
                                diff_subln_w[j].reshape(1, HEAD_W).astype(F32), lambda_init, tiles["tq"],
                                tiles["tk"], tiles["nh"])
            x2 = _proj_ln(o.reshape(t, -1), w_o_b16, j, x2, ln_g[l, 0].reshape(1, d),
                          ln_b[l, 0].reshape(1, d), alpha, tiles["tm_proj"])
        x2 = _mlp_ln(x2, w_up_b16, w_down_b16, l, ln_g[l, 1].reshape(1, d),
                     ln_b[l, 1].reshape(1, d), alpha, tiles["tm_mlp"], tiles["tf_mlp"])
        if l == n_a - 1:
            kv = _matmul(x2, w_kv_b16, 0, BF16, tiles["tm"], tiles["tn"], name="shared_kv_proj")
            kv3 = kv.reshape(bsz, seq, -1)
    return x2.reshape(bsz, seq, d).astype(x.dtype)


def kernel(x, gdn_w_in, gdn_conv_w, gdn_a_log, gdn_dt_bias, gdn_norm_w, gdn_w_out, diff_w_q, diff_lambda,
           diff_subln_w, diff_w_o, shared_w_kv, mlp_w_up, mlp_w_down, ln_g, ln_b):
    return _forward(x, gdn_w_in, gdn_conv_w, gdn_a_log, gdn_dt_bias, gdn_norm_w, gdn_w_out, diff_w_q,
                    diff_lambda, diff_subln_w, diff_w_o, shared_w_kv, mlp_w_up, mlp_w_down, ln_g, ln_b,
                    _tiles())
```

```python
import functools
import math

import jax
import jax.numpy as jnp
from jax import lax
from jax.experimental import pallas as pl
from jax.experimental.pallas import tpu as pltpu

F32 = jnp.float32
BF16 = jnp.bfloat16

HEAD_W = 128
CHUNK = 64
CONV_TAPS = 4
GDN_EPS = 1e-6
SUBLN_EPS = 1e-5
LN_EPS = 1e-5
GATE_W = 128
LN_ROW_CHUNKS = 4

V7X_SUBLANES = 8
V7X_MXU_COLS = 256
ONES_ROWS = 16
V7X_VMEM_LIMIT_BYTES = 56 * 1024 * 1024


def _cparams(sem):
    return pltpu.CompilerParams(dimension_semantics=sem, vmem_limit_bytes=V7X_VMEM_LIMIT_BYTES)


def _bdot(a, b):
    return jnp.dot(a.astype(BF16), b.astype(BF16), preferred_element_type=F32)


def _bdot_nt(a, b):
    return lax.dot_general(a.astype(BF16), b.astype(BF16), (((1,), (1,)), ((), ())),
                           preferred_element_type=F32)


def _bdot_tn(a, b):
    return lax.dot_general(a.astype(BF16), b.astype(BF16), (((0,), (0,)), ((), ())),
                           preferred_element_type=F32)


def _split3(x):
    h = x.astype(BF16)
    r = x - h.astype(F32)
    m = r.astype(BF16)
    l = (r - m.astype(F32)).astype(BF16)
    return h, m, l


def _mm_kernel(x_ref, w_ref, o_ref, *, scale):
    acc = jnp.dot(x_ref[...].astype(BF16), w_ref[...], preferred_element_type=F32)
    if scale != 1.0:
        acc = acc * scale
    o_ref[...] = acc.astype(o_ref.dtype)


def _matmul(x, w, layer, out_dtype, tm, tn, scale=1.0, name="matmul"):
    m, k = x.shape
    n = w.shape[2]
    tm = min(tm, m)
    tn = min(tn, n)
    assert m % tm == 0 and n % tn == 0
    return pl.pallas_call(
        functools.partial(_mm_kernel, scale=scale),
        grid=(m // tm, n // tn),
        in_specs=[pl.BlockSpec((tm, k), lambda i, j: (i, 0)),
                  pl.BlockSpec((None, k, tn), lambda i, j: (layer, 0, j))],
        out_specs=pl.BlockSpec((tm, tn), lambda i, j: (i, j)),
        out_shape=jax.ShapeDtypeStruct((m, n), out_dtype),
        compiler_params=_cparams(("parallel", "parallel")),
        name=name,
    )(x, w)


def _gate_kernel(x_ref, w_ref, isa_ref, nega_ref, dtb_ref, o_ref):
    x = x_ref[...]
    w = w_ref[...]
    xh = x.astype(BF16)
    xl = (x - xh.astype(F32)).astype(BF16)
    wh = w.astype(BF16)
    wl = (w - wh.astype(F32)).astype(BF16)
    acc = (jnp.dot(xh, wh, preferred_element_type=F32)
           + jnp.dot(xl, wh, preferred_element_type=F32)
           + jnp.dot(xh, wl, preferred_element_type=F32))
    beta = jax.nn.sigmoid(acc)
    s = acc + dtb_ref[...]
    softplus = jnp.maximum(s, 0.0) + jnp.log1p(jnp.exp(-jnp.abs(s)))
    g = nega_ref[...] * softplus
    o_ref[...] = jnp.where(isa_ref[...] > 0.5, g, beta)


def _gates(x2, w_gate, is_a, neg_a, dtb, tm):
    t, d = x2.shape
    tm = min(tm, t)
    vec = pl.BlockSpec((1, GATE_W), lambda i: (0, 0))
    return pl.pallas_call(
        _gate_kernel,
        grid=(t // tm,),
        in_specs=[pl.BlockSpec((tm, d), lambda i: (i, 0)),
                  pl.BlockSpec((d, GATE_W), lambda i: (0, 0)), vec, vec, vec],
        out_specs=pl.BlockSpec((tm, GATE_W), lambda i: (i, 0)),
        out_shape=jax.ShapeDtypeStruct((t, GATE_W), F32),
        compiler_params=_cparams(("parallel",)),
        name="gdn_gates",
    )(x2, w_gate, is_a, neg_a, dtb)


def _each(fn, *lists):
    return [fn(*args) for args in zip(*lists)]


def _unit_lower_inverse_minus_identity(lows, same16, same32):
    c = lows[0].shape[0]
    d = _each(lambda low: jnp.where(same16, low, 0.0), lows)
    x = _each(lambda a: -a, d)
    p = _each(lambda a: _bdot(a, a), d)
    xp = _each(lambda a, b: _bdot(jnp.concatenate([a, b], axis=0), b), x, p)
    n = _each(lambda a, b, ab: a + b + ab[:c], x, p, xp)
    p2 = _each(lambda ab: ab[c:], xp)
    np2 = _each(lambda a, b: _bdot(jnp.concatenate([a, b], axis=0), b), n, p2)
    n = _each(lambda a, b, ab: a + b + ab[:c], n, p2, np2)
    p4 = _each(lambda ab: ab[c:], np2)
    n = _each(lambda a, b: a + b + _bdot(a, b), n, p4)
    only32 = jnp.logical_and(same32, jnp.logical_not(same16))
    off1 = _each(lambda low: jnp.where(only32, low, 0.0), lows)
    c1 = _each(lambda a, o: o + _bdot(a, o), n, off1)
    n = _each(lambda a, cc: a - (cc + _bdot(cc, a)), n, c1)
    off2 = _each(lambda low: jnp.where(same32, 0.0, low), lows)
    c2 = _each(lambda a, o: o + _bdot(a, o), n, off2)
    n = _each(lambda a, cc: a - (cc + _bdot(cc, a)), n, c2)
    return n


def _gdn_kernel(pq_ref, pk_ref, pv_ref, pz_ref, cwq_ref, cwk_ref, cwv_ref, gcol_ref, grow_ref,
                nw_ref, o_ref, hq_ref, hk_ref, hv_ref, state_ref, *, hb):
    c_idx = pl.program_id(2)
    C = CHUNK
    w_blk = hb * HEAD_W

    @pl.when(c_idx == 0)
    def _():
        state_ref[...] = jnp.zeros_like(state_ref)
        zero = jnp.zeros((C, w_blk), BF16)
        hq_ref[0:C, :] = zero
        hk_ref[0:C, :] = zero
        hv_ref[0:C, :] = zero

    sr = lax.broadcasted_iota(jnp.int32, ((CONV_TAPS - 1) * C, 2 * C), 0)
    sc = lax.broadcasted_iota(jnp.int32, ((CONV_TAPS - 1) * C, 2 * C), 1)
    tap = sr // C
    shift_mat = jnp.where(sc == sr - tap * (C - 1) + (C - (CONV_TAPS - 1)), 1.0, 0.0).astype(BF16)

    def conv_silu(p_ref, h_ref, cw_ref, rows):
        cur = p_ref[rows, :]
        h_ref[C:2 * C, :] = cur
        shifted = jnp.dot(shift_mat, h_ref[...], preferred_element_type=F32)
        w = cw_ref[...]
        y = shifted[0:C] * w[0:1, :]
        for j in range(1, CONV_TAPS - 1):
            y = y + shifted[j * C:(j + 1) * C] * w[j:j + 1, :]
        y = y + cur.astype(F32) * w[CONV_TAPS - 1:CONV_TAPS, :]
        h_ref[0:C, :] = cur
        return y * jax.nn.sigmoid(y)

    row = lax.broadcasted_iota(jnp.int32, (C, C), 0)
    col = lax.broadcasted_iota(jnp.int32, (C, C), 1)
    causal = row >= col
    strict = row > col
    same16 = (row // 16) == (col // 16)
    same32 = (row // 32) == (col // 32)
    tril = jnp.where(causal, 1.0, 0.0).astype(BF16)
    triu = jnp.where(row <= col, 1.0, 0.0).astype(BF16)

    nw = nw_ref[...]
    q_scale = HEAD_W ** -0.5

    def l2n(y, scale):
        return y * (lax.rsqrt(jnp.sum(y * y, axis=-1, keepdims=True) + GDN_EPS) * scale)

    def chunk(cc, carry):
        rows = pl.ds(pl.multiple_of(cc * C, C), C)
        yq = conv_silu(pq_ref, hq_ref, cwq_ref, rows)
        yk = conv_silu(pk_ref, hk_ref, cwk_ref, rows)
        yv = conv_silu(pv_ref, hv_ref, cwv_ref, rows)

        gcol = gcol_ref[rows, :]
        grow = grow_ref[cc]
        gh, gm, gl = _split3(gcol[:, hb:])
        gc_col = (jnp.dot(tril, gh, preferred_element_type=F32)
                  + jnp.dot(tril, gm, preferred_element_type=F32)
                  + jnp.dot(tril, gl, preferred_element_type=F32))
        rh, rm, rl = _split3(grow[hb:, :])
        gc_row = (jnp.dot(rh, triu, preferred_element_type=F32)
                  + jnp.dot(rm, triu, preferred_element_type=F32)
                  + jnp.dot(rl, triu, preferred_element_type=F32))

        heads = list(range(hb))
        sl = [slice(h * HEAD_W, (h + 1) * HEAD_W) for h in heads]
        q = [l2n(yq[:, s], q_scale) for s in sl]
        k = [l2n(yk[:, s], 1.0) for s in sl]
        v = [yv[:, s] for s in sl]
        beta = [gcol[:, h:h + 1] for h in heads]
        g_c = [gc_col[:, h:h + 1] for h in heads]
        g_r = [gc_row[h:h + 1, :] for h in heads]
        g_last = [r[:, C - 1:C] for r in g_r]
        decay = _each(lambda gc, gr: jnp.exp(jnp.where(causal, gc - gr, -jnp.inf)), g_c, g_r)
        eg = _each(jnp.exp, g_c)
        k_beta = _each(lambda a, b: a * b, k, beta)
        kq = _each(lambda kb, qq, kk: _bdot_nt(jnp.concatenate([kb, qq], axis=0), kk), k_beta, q, k)
        low = _each(lambda a, dc: jnp.where(strict, a[:C] * dc, 0.0), kq, decay)
        attn = _each(lambda a, dc: a[C:] * dc, kq, decay)
        n_inv = _unit_lower_inverse_minus_identity(low, same16, same32)
        rhs = _each(lambda vv, b, kb, e: jnp.concatenate([vv * b, kb * e], axis=1), v, beta, k_beta, eg)
        sol = _each(lambda r, n: r + _bdot(n, r), rhs, n_inv)
        state = [state_ref[h] for h in heads]
        ws = _each(lambda s, qq, e, st: _bdot(jnp.concatenate([s[:, HEAD_W:], qq * e], axis=0), st),
                   sol, q, eg, state)
        v_new = _each(lambda s, a: s[:, :HEAD_W] - a[:C], sol, ws)
        o = _each(lambda a, at, vn: a[C:] + _bdot(at, vn), ws, attn, v_new)
        k_dec = _each(lambda kk, gl_, gc: kk * jnp.exp(gl_ - gc), k, g_last, g_c)
        new_state = _each(lambda st, gl_, kd, vn: st * jnp.exp(gl_) + _bdot_tn(kd, vn),
                          state, g_last, k_dec, v_new)
        for h, st in zip(heads, new_state):
            state_ref[h] = st
        o = _each(lambda a: a * lax.rsqrt(jnp.mean(a * a, axis=-1, keepdims=True) + GDN_EPS) * nw, o)
        for s, a in zip(sl, o):
            zh = pz_ref[rows, s].astype(F32)
            o_ref[rows, s] = (a * (zh * jax.nn.sigmoid(zh))).astype(o_ref.dtype)
        return carry

    lax.fori_loop(0, pq_ref.shape[0] // C, chunk, 0)


def _gdn_core(p3, conv_w, gcol, grow, norm_w, hb, chunks_per_step):
    b, s, w4 = p3.shape
    assert p3.dtype == BF16
    nck = min(chunks_per_step, s // CHUNK)
    tc = nck * CHUNK
    assert s % tc == 0
    v_w = w4 // 4
    heads = v_w // HEAD_W
    groups = heads // hb
    wb = hb * HEAD_W

    def pspec(part):
        return pl.BlockSpec((None, tc, wb), lambda bi, gi, ci, part=part: (bi, ci, part * groups + gi))

    def cspec(part):
        return pl.BlockSpec((CONV_TAPS, wb), lambda bi, gi, ci, part=part: (0, part * groups + gi))

    halo = pltpu.VMEM((2 * CHUNK, wb), BF16)
    return pl.pallas_call(
        functools.partial(_gdn_kernel, hb=hb),
        grid=(b, groups, s // tc),
        in_specs=[pspec(0), pspec(1), pspec(2), pspec(3), cspec(0), cspec(1), cspec(2),
                  pl.BlockSpec((None, None, tc, 2 * hb), lambda bi, gi, ci: (bi, gi, ci, 0)),
                  pl.BlockSpec((None, None, nck, 2 * hb, CHUNK), lambda bi, gi, ci: (bi, gi, ci, 0, 0)),
                  pl.BlockSpec((1, HEAD_W), lambda bi, gi, ci: (0, 0))],
        out_specs=pl.BlockSpec((None, tc, wb), lambda bi, gi, ci: (bi, ci, gi)),
        out_shape=jax.ShapeDtypeStruct((b, s, v_w), BF16),
        scratch_shapes=[halo, halo, halo, pltpu.VMEM((hb, HEAD_W, HEAD_W), F32)],
        compiler_params=_cparams(("parallel", "parallel", "arbitrary")),
        name="gdn_core",
    )(p3, p3, p3, p3, conv_w, conv_w, conv_w, gcol, grow, norm_w)


def _layer_norm_rows(y, g, b):
    mu = jnp.mean(y, axis=-1, keepdims=True)
    yc = y - mu
    var = jnp.mean(yc * yc, axis=-1, keepdims=True)
    return yc * lax.rsqrt(var + LN_EPS) * g + b


def _row_chunks(rows):
    n = LN_ROW_CHUNKS if rows % (LN_ROW_CHUNKS * V7X_SUBLANES) == 0 else 1
    step = rows // n
    return [slice(k * step, (k + 1) * step) for k in range(n)]


def _matmul_then_ln(chunks, matmul_rows, ln_store_rows):
    pending = None
    for rs in chunks:
        h = matmul_rows(rs)
        if pending is not None:
            ln_store_rows(*pending)
        pending = (rs, h)
    ln_store_rows(*pending)


def _proj_ln_kernel(a_ref, w_ref, x_ref, g_ref, b_ref, o_ref, *, alpha):
    def matmul_rows(rs):
        return jnp.dot(a_ref[rs, :].astype(BF16), w_ref[...], preferred_element_type=F32)

    def ln_store_rows(rs, h):
        o_ref[rs, :] = _layer_norm_rows(alpha * x_ref[rs, :] + h, g_ref[...], b_ref[...])

    _matmul_then_ln(_row_chunks(a_ref.shape[0]), matmul_rows, ln_store_rows)


def _proj_ln(a, w, layer, x2, g, b, alpha, tm):
    t, k = a.shape
    d = w.shape[2]
    tm = min(tm, t)
    vec = pl.BlockSpec((1, d), lambda i: (0, 0))
    return pl.pallas_call(
        functools.partial(_proj_ln_kernel, alpha=alpha),
        grid=(t // tm,),
        in_specs=[pl.BlockSpec((tm, k), lambda i: (i, 0)),
                  pl.BlockSpec((None, k, d), lambda i: (layer, 0, 0)),
                  pl.BlockSpec((tm, d), lambda i: (i, 0)), vec, vec],
        out_specs=pl.BlockSpec((tm, d), lambda i: (i, 0)),
        out_shape=jax.ShapeDtypeStruct((t, d), F32),
        compiler_params=_cparams(("parallel",)),
        name="proj_ln",
    )(a, w, x2, g, b)


def _mlp_ln_kernel(x_ref, wu_ref, wd_ref, g_ref, b_ref, o_ref, xb_ref, acc_ref, *, alpha):
    f = pl.program_id(1)

    @pl.when(f == 0)
    def _():
        xb_ref[...] = x_ref[...].astype(BF16)
        acc_ref[...] = jnp.zeros_like(acc_ref)

    h = jnp.dot(xb_ref[...], wu_ref[...], preferred_element_type=F32)
    h = jnp.maximum(h, 0.0)
    h = h * h
    acc_ref[...] += jnp.dot(h.astype(BF16), wd_ref[...], preferred_element_type=F32)

    @pl.when(f == pl.num_programs(1) - 1)
    def _():
        o_ref[...] = _layer_norm_rows(alpha * x_ref[...] + acc_ref[...], g_ref[...], b_ref[...])


def _mlp_ln(x2, w_up, w_down, layer, g, b, alpha, tm, tf):
    t, d = x2.shape
    ff = w_up.shape[2]
    tm = min(tm, t)
    tf = min(tf, ff)
    vec = pl.BlockSpec((1, d), lambda i, f: (0, 0))
    return pl.pallas_call(
        functools.partial(_mlp_ln_kernel, alpha=alpha),
        grid=(t // tm, ff // tf),
        in_specs=[pl.BlockSpec((tm, d), lambda i, f: (i, 0)),
                  pl.BlockSpec((None, d, tf), lambda i, f: (layer, 0, f)),
                  pl.BlockSpec((None, tf, d), lambda i, f: (layer, f, 0)), vec, vec],
        out_specs=pl.BlockSpec((tm, d), lambda i, f: (i, 0)),
        out_shape=jax.ShapeDtypeStruct((t, d), F32),
        scratch_shapes=[pltpu.VMEM((tm, d), BF16), pltpu.VMEM((tm, d), F32)],
        compiler_params=_cparams(("parallel", "arbitrary")),
        name="mlp_ln",
    )(x2, w_up, w_down, g, b)


def _diff_attn_kernel(q_ref, k_ref, v_ref, lam_ref, sw_ref, o_ref, qs_ref, s_ref, acc_ref,
                      *, tq, tk, nh, lambda_init):
    nq = q_ref.shape[0] // tq
    heads = list(range(nh))
    hs = [slice(h * HEAD_W, (h + 1) * HEAD_W) for h in heads]
    half = HEAD_W // 2
    lane = lax.broadcasted_iota(jnp.int32, (tq, HEAD_W), 1)
    cw = min(V7X_MXU_COLS, tq)
    assert tq % cw == 0
    ones_rows = jnp.ones((ONES_ROWS, tk), BF16)

    def stack_queries(i):
        qrows = pl.ds(pl.multiple_of(i * tq, tq), tq)
        for h in heads:
            q = q_ref[qrows, hs[h]]
            zero = jnp.zeros_like(q)
            qs_ref[h, 0:tq, :] = jnp.where(lane < half, q, zero)
            qs_ref[h, tq:2 * tq, :] = jnp.where(lane >= half, q, zero)

    def produce(slot, j, diag=None):
        rows = pl.ds(pl.multiple_of(j * tk, tk), tk)
        first = 0 if diag is None else diag * tk
        for h in heads:
            kb = k_ref[rows, hs[h]]
            for lo, hi in ([(0, 2 * tq)] if first == 0 else [(first, tq), (tq + first, 2 * tq)]):
                s_ref[h, slot, :, lo:hi] = lax.dot_general(kb, qs_ref[h, lo:hi, :], (((1,), (1,)), ((), ())),
                                                           preferred_element_type=F32)

    def consume(slot, j, m_prev, diag):
        rows = pl.ds(pl.multiple_of(j * tk, tk), tk)
        lhs = [jnp.concatenate([v_ref[rows, hs[h]].T, ones_rows], axis=0) for h in heads]
        m_parts = [[] for _ in heads]
        for c in range(2 * tq // cw):
            cs = slice(c * cw, (c + 1) * cw)
            r_min = (c * cw) % tq
            if diag is not None and diag * tk > r_min + cw - 1:
                for h in heads:
                    m_parts[h].append(m_prev[h][:, cs])
                continue
            nk = tk if diag is None else min(tk, r_min + cw - diag * tk)
            for h in heads:
                m_prev_c = m_prev[h][:, cs]
                sc = s_ref[h, slot, 0:nk, cs]
                if diag is not None and diag * tk + nk - 1 > r_min:
                    key = lax.broadcasted_iota(jnp.int32, (nk, cw), 0) + diag * tk
                    qry = lax.broadcasted_iota(jnp.int32, (nk, cw), 1) + r_min
                    sc = jnp.where(key <= qry, sc, -jnp.inf)
                m_new_c = jnp.maximum(m_prev_c, jnp.max(sc, axis=0, keepdims=True))
                alpha_c = jnp.exp2(m_prev_c - m_new_c)
                p_c = jnp.exp2(sc - m_new_c).astype(BF16)
                pv_c = jnp.dot(lhs[h][:, 0:nk], p_c, preferred_element_type=F32)
                acc_ref[h, :, cs] = alpha_c * acc_ref[h, :, cs] + pv_c
                m_parts[h].append(m_new_c)
        return tuple(jnp.concatenate(parts, axis=1) for parts in m_parts)

    assert tq == 2 * tk

    def pair(jj, m):
        produce(1, 2 * jj + 1)
        m = consume(0, 2 * jj, m, None)
        produce(0, 2 * jj + 2)
        return consume(1, 2 * jj + 1, m, None)

    def attend(i):
        m = lax.fori_loop(0, i, pair, tuple(jnp.full((1, 2 * tq), -jnp.inf, F32) for _ in heads))
        produce(1, 2 * i + 1, 1)
        m = consume(0, 2 * i, m, 0)
        consume(1, 2 * i + 1, m, 1)

    lp = lam_ref[...]
    lam = (jnp.exp(jnp.sum(lp[0:1, :] * lp[1:2, :], axis=-1, keepdims=True))
           - jnp.exp(jnp.sum(lp[2:3, :] * lp[3:4, :], axis=-1, keepdims=True)) + lambda_init)

    def finalize(i):
        qrows = pl.ds(pl.multiple_of(i * tq, tq), tq)
        o_all = [acc_ref[h, 0:HEAD_W, :] * (1.0 / acc_ref[h, HEAD_W:HEAD_W + 1, :]) for h in heads]
        o = [(a[:, 0:tq] - lam * a[:, tq:2 * tq]).T for a in o_all]
        o = [a * lax.rsqrt(jnp.mean(a * a, axis=-1, keepdims=True) + SUBLN_EPS) * sw_ref[...] for a in o]
        for h in heads:
            o_ref[qrows, hs[h]] = (o[h] * (1.0 - lambda_init)).astype(o_ref.dtype)

    acc_ref[...] = jnp.ones_like(acc_ref)

    def query_block(i, carry):
        stack_queries(i)
        produce(0, 0)
        finalize(jnp.maximum(i - 1, 0))
        acc_ref[...] = jnp.zeros_like(acc_ref)
        attend(i)
        return carry

    lax.fori_loop(0, nq, query_block, 0)
    finalize(nq - 1)


def _diff_attention(q3, kv3, lam_params, subln_w, lambda_init, tq, tk, nh):
    b, s, wq = q3.shape
    heads = wq // HEAD_W
    tq = min(tq, s)
    tk = min(tk, tq)
    nh = min(nh, heads)
    groups = heads // nh
    wb = nh * HEAD_W
    return pl.pallas_call(
        functools.partial(_diff_attn_kernel, tq=tq, tk=tk, nh=nh, lambda_init=lambda_init),
        grid=(b, groups),
        in_specs=[pl.BlockSpec((None, s, wb), lambda bi, g: (bi, 0, g)),
                  pl.BlockSpec((None, s, wb), lambda bi, g: (bi, 0, g)),
                  pl.BlockSpec((None, s, wb), lambda bi, g, groups=groups: (bi, 0, groups + g)),
                  pl.BlockSpec(lam_params.shape, lambda bi, g: (0, 0)),
                  pl.BlockSpec((1, HEAD_W), lambda bi, g: (0, 0))],
        out_specs=pl.BlockSpec((None, s, wb), lambda bi, g: (bi, 0, g)),
        out_shape=jax.ShapeDtypeStruct((b, s, wq), BF16),
        scratch_shapes=[pltpu.VMEM((nh, 2 * tq, HEAD_W), BF16),
                        pltpu.VMEM((nh, 2, tk, 2 * tq), F32),
                        pltpu.VMEM((nh, HEAD_W + ONES_ROWS, 2 * tq), F32)],
        compiler_params=_cparams(("parallel", "parallel")),
        name="diff_attn",
    )(q3, kv3, kv3, lam_params, subln_w)


def _gdn_layer(x2, bsz, seq, layer, w_in_f32, w_in_b16, conv_w, a_log, dt_bias, norm_w, w_out_b16, ln_g, ln_b,
               alpha, hb, tiles):
    t, d = x2.shape
    heads = a_log.shape[0]
    v_w = heads * HEAD_W
    groups = heads // hb
    main = 4 * v_w
    p = _matmul(x2, w_in_b16, layer, BF16, tiles["tm"], tiles["tn"], name="gdn_in_proj")
    wb = w_in_f32[layer, :, main:main + heads].reshape(d, groups, hb)
    wa = w_in_f32[layer, :, main + heads:main + 2 * heads].reshape(d, groups, hb)
    w_gate = jnp.concatenate([wb, wa], axis=2).reshape(d, 2 * heads)
    w_gate = jnp.pad(w_gate, ((0, 0), (0, GATE_W - 2 * heads)))

    def per_col(vals, fill):
        zeros = jnp.full((groups, hb), fill, F32)
        cols = jnp.concatenate([zeros, vals.astype(F32).reshape(groups, hb)], axis=1).reshape(1, 2 * heads)
        return jnp.pad(cols, ((0, 0), (0, GATE_W - 2 * heads)), constant_values=fill)

    is_a = per_col(jnp.ones((heads,), F32), 0.0)
    neg_a = per_col(-jnp.exp(a_log.astype(F32)), 0.0)
    dtb = per_col(dt_bias, 0.0)
    gates = _gates(x2, w_gate, is_a, neg_a, dtb, tiles["tm_gate"])[:, :2 * heads]
    gcol = gates.reshape(bsz, seq, groups, 2 * hb).transpose(0, 2, 1, 3)
    grow = gates.reshape(bsz, seq // CHUNK, CHUNK, groups, 2 * hb).transpose(0, 3, 1, 4, 2)
    o = _gdn_core(p.reshape(bsz, seq, main), conv_w, gcol, grow, norm_w.reshape(1, HEAD_W), hb,
                  tiles["gdn_chunks"])
    return _proj_ln(o.reshape(t, v_w), w_out_b16, layer, x2, ln_g.reshape(1, d), ln_b.reshape(1, d),
                    alpha, tiles["tm_proj"])


def _tiles():
    return dict(tm=1024, tn=2048, tm_gate=512, tm_proj=512, tm_mlp=512, tf_mlp=1024, tq=1024, tk=512, nh=2, hb=16,
                gdn_chunks=8)


def _forward(x, gdn_w_in, gdn_conv_w, gdn_a_log, gdn_dt_bias, gdn_norm_w, gdn_w_out, diff_w_q, diff_lambda,
             diff_subln_w, diff_w_o, shared_w_kv, mlp_w_up, mlp_w_down, ln_g, ln_b, tiles):
    bsz, seq, d = x.shape
    t = bsz * seq
    depth = mlp_w_up.shape[0]
    n_a = gdn_w_in.shape[0]
    alpha = (2 * depth) ** 0.25
    x2 = x.reshape(t, d).astype(F32)
    w_in_b16 = gdn_w_in[:, :, :4 * gdn_a_log.shape[1] * HEAD_W].astype(BF16)
    w_out_b16 = gdn_w_out.astype(BF16)
    w_q_b16 = diff_w_q.astype(BF16)
    w_o_b16 = diff_w_o.astype(BF16)
    w_kv_b16 = shared_w_kv.astype(BF16)[None]
    w_up_b16 = mlp_w_up.astype(BF16)
    w_down_b16 = mlp_w_down.astype(BF16)
    kv3 = None
    for l in range(depth):
        if l < n_a:
            x2 = _gdn_layer(x2, bsz, seq, l, gdn_w_in, w_in_b16, gdn_conv_w[l], gdn_a_log[l], gdn_dt_bias[l],
                            gdn_norm_w[l], w_out_b16, ln_g[l, 0], ln_b[l, 0], alpha,
                            min(tiles["hb"], gdn_a_log.shape[1]), tiles)
        else:
            j = l - n_a
            lambda_init = 0.8 - 0.6 * math.exp(-0.3 * l)
            dk = diff_lambda.shape[-1]
            q = _matmul(x2, w_q_b16, j, BF16, tiles["tm"], tiles["tn"],
                        scale=dk ** -0.5 * math.log2(math.e), name="diff_q_proj")
            o = _diff_attention(q.reshape(bsz, seq, -1), kv3, diff_lambda[j].astype(F32),
                                diff_subln_w[j].reshape(1, HEAD_W).astype(F32), lambda_init, tiles["tq"],
                                tiles["tk"], tiles["nh"])
            x2 = _proj_ln(o.reshape(t, -1), w_o_b16, j, x2, ln_g[l, 0].reshape(1, d),
                          ln_b[l, 0].reshape(1, d), alpha, tiles["tm_proj"])
        x2 = _mlp_ln(x2, w_up_b16, w_down_b16, l, ln_g[l, 1].reshape(1, d),
                     ln_b[l, 1].reshape(1, d), alpha, tiles["tm_mlp"], tiles["tf_mlp"])
        if l == n_a - 1:
            kv = _matmul(x2, w_kv_b16, 0, BF16, tiles["tm"], tiles["tn"], name="shared_kv_proj")
            kv3 = kv.reshape(bsz, seq, -1)
    return x2.reshape(bsz, seq, d).astype(x.dtype)


def kernel(x, gdn_w_in, gdn_conv_w, gdn_a_log, gdn_dt_bias, gdn_norm_w, gdn_w_out, diff_w_q, diff_lambda,
           diff_subln_w, diff_w_o, shared_w_kv, mlp_w_up, mlp_w_down, ln_g, ln_b):
    return _forward(x, gdn_w_in, gdn_conv_w, gdn_a_log, gdn_dt_bias, gdn_norm_w, gdn_w_out, diff_w_q,
                    diff_lambda, diff_subln_w, diff_w_o, shared_w_kv, mlp_w_up, mlp_w_down, ln_g, ln_b,
                    _tiles())
```

```python
import functools
import math

import jax
import jax.numpy as jnp
from jax import lax
from jax.experimental import pallas as pl
from jax.experimental.pallas import tpu as pltpu

F32 = jnp.float32
BF16 = jnp.bfloat16

HEAD_W = 128
CHUNK = 64
CONV_TAPS = 4
GDN_EPS = 1e-6
SUBLN_EPS = 1e-5
LN_EPS = 1e-5
GATE_W = 128
LN_ROW_CHUNKS = 4

V7X_SUBLANES = 8
V7X_MXU_COLS = 256
ONES_ROWS = 16
V7X_VMEM_LIMIT_BYTES = 56 * 1024 * 1024


def _cparams(sem):
    return pltpu.CompilerParams(dimension_semantics=sem, vmem_limit_bytes=V7X_VMEM_LIMIT_BYTES)


def _bdot(a, b):
    return jnp.dot(a.astype(BF16), b.astype(BF16), preferred_element_type=F32)


def _bdot_nt(a, b):
    return lax.dot_general(a.astype(BF16), b.astype(BF16), (((1,), (1,)), ((), ())),
                           preferred_element_type=F32)


def _bdot_tn(a, b):
    return lax.dot_general(a.astype(BF16), b.astype(BF16), (((0,), (0,)), ((), ())),
                           preferred_element_type=F32)


def _split3(x):
    h = x.astype(BF16)
    r = x - h.astype(F32)
    m = r.astype(BF16)
    l = (r - m.astype(F32)).astype(BF16)
    return h, m, l


def _mm_kernel(x_ref, w_ref, o_ref, *, scale):
    acc = jnp.dot(x_ref[...].astype(BF16), w_ref[...], preferred_element_type=F32)
    if scale != 1.0:
        acc = acc * scale
    o_ref[...] = acc.astype(o_ref.dtype)


def _matmul(x, w, layer, out_dtype, tm, tn, scale=1.0, name="matmul"):
    m, k = x.shape
    n = w.shape[2]
    tm = min(tm, m)
    tn = min(tn, n)
    assert m % tm == 0 and n % tn == 0
    return pl.pallas_call(
        functools.partial(_mm_kernel, scale=scale),
        grid=(m // tm, n // tn),
        in_specs=[pl.BlockSpec((tm, k), lambda i, j: (i, 0)),
                  pl.BlockSpec((None, k, tn), lambda i, j: (layer, 0, j))],
        out_specs=pl.BlockSpec((tm, tn), lambda i, j: (i, j)),
        out_shape=jax.ShapeDtypeStruct((m, n), out_dtype),
        compiler_params=_cparams(("parallel", "parallel")),
        name=name,
    )(x, w)


def _gate_kernel(x_ref, w_ref, isa_ref, nega_ref, dtb_ref, o_ref):
    acc = jnp.dot(x_ref[...].astype(BF16), w_ref[...].astype(BF16), preferred_element_type=F32)
    beta = jax.nn.sigmoid(acc)
    s = acc + dtb_ref[...]
    softplus = jnp.maximum(s, 0.0) + jnp.log1p(jnp.exp(-jnp.abs(s)))
    g = nega_ref[...] * softplus
    o_ref[...] = jnp.where(isa_ref[...] > 0.5, g, beta)


def _gates(x2, w_gate, is_a, neg_a, dtb, tm):
    t, d = x2.shape
    tm = min(tm, t)
    vec = pl.BlockSpec((1, GATE_W), lambda i: (0, 0))
    return pl.pallas_call(
        _gate_kernel,
        grid=(t // tm,),
        in_specs=[pl.BlockSpec((tm, d), lambda i: (i, 0)),
                  pl.BlockSpec((d, GATE_W), lambda i: (0, 0)), vec, vec, vec],
        out_specs=pl.BlockSpec((tm, GATE_W), lambda i: (i, 0)),
        out_shape=jax.ShapeDtypeStruct((t, GATE_W), F32),
        compiler_params=_cparams(("parallel",)),
        name="gdn_gates",
    )(x2, w_gate, is_a, neg_a, dtb)


def _each(fn, *lists):
    return [fn(*args) for args in zip(*lists)]


def _unit_lower_inverse_minus_identity(lows, same16, same32):
    c = lows[0].shape[0]
    d = _each(lambda low: jnp.where(same16, low, 0.0), lows)
    x = _each(lambda a: -a, d)
    p = _each(lambda a: _bdot(a, a), d)
    xp = _each(lambda a, b: _bdot(jnp.concatenate([a, b], axis=0), b), x, p)
    n = _each(lambda a, b, ab: a + b + ab[:c], x, p, xp)
    p2 = _each(lambda ab: ab[c:], xp)
    np2 = _each(lambda a, b: _bdot(jnp.concatenate([a, b], axis=0), b), n, p2)
    n = _each(lambda a, b, ab: a + b + ab[:c], n, p2, np2)
    p4 = _each(lambda ab: ab[c:], np2)
    n = _each(lambda a, b: a + b + _bdot(a, b), n, p4)
    only32 = jnp.logical_and(same32, jnp.logical_not(same16))
    off1 = _each(lambda low: jnp.where(only32, low, 0.0), lows)
    c1 = _each(lambda a, o: o + _bdot(a, o), n, off1)
    n = _each(lambda a, cc: a - (cc + _bdot(cc, a)), n, c1)
    off2 = _each(lambda low: jnp.where(same32, 0.0, low), lows)
    c2 = _each(lambda a, o: o + _bdot(a, o), n, off2)
    n = _each(lambda a, cc: a - (cc + _bdot(cc, a)), n, c2)
    return n


def _gdn_kernel(pq_ref, pk_ref, pv_ref, pz_ref, cwq_ref, cwk_ref, cwv_ref, gcol_ref, grow_ref,
                nw_ref, o_ref, hq_ref, hk_ref, hv_ref, state_ref, *, hb):
    c_idx = pl.program_id(2)
    C = CHUNK
    w_blk = hb * HEAD_W

    @pl.when(c_idx == 0)
    def _():
        state_ref[...] = jnp.zeros_like(state_ref)
        zero = jnp.zeros((C, w_blk), BF16)
        hq_ref[0:C, :] = zero
        hk_ref[0:C, :] = zero
        hv_ref[0:C, :] = zero

    sr = lax.broadcasted_iota(jnp.int32, ((CONV_TAPS - 1) * C, 2 * C), 0)
    sc = lax.broadcasted_iota(jnp.int32, ((CONV_TAPS - 1) * C, 2 * C), 1)
    tap = sr // C
    shift_mat = jnp.where(sc == sr - tap * (C - 1) + (C - (CONV_TAPS - 1)), 1.0, 0.0).astype(BF16)

    def conv_silu(p_ref, h_ref, cw_ref, rows):
        cur = p_ref[rows, :]
        h_ref[C:2 * C, :] = cur
        shifted = jnp.dot(shift_mat, h_ref[...], preferred_element_type=F32)
        w = cw_ref[...]
        y = shifted[0:C] * w[0:1, :]
        for j in range(1, CONV_TAPS - 1):
            y = y + shifted[j * C:(j + 1) * C] * w[j:j + 1, :]
        y = y + cur.astype(F32) * w[CONV_TAPS - 1:CONV_TAPS, :]
        h_ref[0:C, :] = cur
        return y * jax.nn.sigmoid(y)

    row = lax.broadcasted_iota(jnp.int32, (C, C), 0)
    col = lax.broadcasted_iota(jnp.int32, (C, C), 1)
    causal = row >= col
    strict = row > col
    same16 = (row // 16) == (col // 16)
    same32 = (row // 32) == (col // 32)
    tril = jnp.where(causal, 1.0, 0.0).astype(BF16)
    triu = jnp.where(row <= col, 1.0, 0.0).astype(BF16)

    nw = nw_ref[...]
    q_scale = HEAD_W ** -0.5

    def l2n(y, scale):
        return y * (lax.rsqrt(jnp.sum(y * y, axis=-1, keepdims=True) + GDN_EPS) * scale)

    def chunk(cc, carry):
        rows = pl.ds(pl.multiple_of(cc * C, C), C)
        yq = conv_silu(pq_ref, hq_ref, cwq_ref, rows)
        yk = conv_silu(pk_ref, hk_ref, cwk_ref, rows)
        yv = conv_silu(pv_ref, hv_ref, cwv_ref, rows)

        gcol = gcol_ref[rows, :]
        grow = grow_ref[cc]
        gh, gm, gl = _split3(gcol[:, hb:])
        gc_col = (jnp.dot(tril, gh, preferred_element_type=F32)
                  + jnp.dot(tril, gm, preferred_element_type=F32)
                  + jnp.dot(tril, gl, preferred_element_type=F32))
        rh, rm, rl = _split3(grow[hb:, :])
        gc_row = (jnp.dot(rh, triu, preferred_element_type=F32)
                  + jnp.dot(rm, triu, preferred_element_type=F32)
                  + jnp.dot(rl, triu, preferred_element_type=F32))

        heads = list(range(hb))
        sl = [slice(h * HEAD_W, (h + 1) * HEAD_W) for h in heads]
        q = [l2n(yq[:, s], q_scale) for s in sl]
        k = [l2n(yk[:, s], 1.0) for s in sl]
        v = [yv[:, s] for s in sl]
        beta = [gcol[:, h:h + 1] for h in heads]
        g_c = [gc_col[:, h:h + 1] for h in heads]
        g_r = [gc_row[h:h + 1, :] for h in heads]
        g_last = [r[:, C - 1:C] for r in g_r]
        decay = _each(lambda gc, gr: jnp.exp(jnp.where(causal, gc - gr, -jnp.inf)), g_c, g_r)
        eg = _each(jnp.exp, g_c)
        k_beta = _each(lambda a, b: a * b, k, beta)
        kq = _each(lambda kb, qq, kk: _bdot_nt(jnp.concatenate([kb, qq], axis=0), kk), k_beta, q, k)
        low = _each(lambda a, dc: jnp.where(strict, a[:C] * dc, 0.0), kq, decay)
        attn = _each(lambda a, dc: a[C:] * dc, kq, decay)
        n_inv = _unit_lower_inverse_minus_identity(low, same16, same32)
        rhs = _each(lambda vv, b, kb, e: jnp.concatenate([vv * b, kb * e], axis=1), v, beta, k_beta, eg)
        sol = _each(lambda r, n: r + _bdot(n, r), rhs, n_inv)
        state = [state_ref[h] for h in heads]
        ws = _each(lambda s, qq, e, st: _bdot(jnp.concatenate([s[:, HEAD_W:], qq * e], axis=0), st),
                   sol, q, eg, state)
        v_new = _each(lambda s, a: s[:, :HEAD_W] - a[:C], sol, ws)
        o = _each(lambda a, at, vn: a[C:] + _bdot(at, vn), ws, attn, v_new)
        k_dec = _each(lambda kk, gl_, gc: kk * jnp.exp(gl_ - gc), k, g_last, g_c)
        new_state = _each(lambda st, gl_, kd, vn: st * jnp.exp(gl_) + _bdot_tn(kd, vn),
                          state, g_last, k_dec, v_new)
        for h, st in zip(heads, new_state):
            state_ref[h] = st
        o = _each(lambda a: a * lax.rsqrt(jnp.mean(a * a, axis=-1, keepdims=True) + GDN_EPS) * nw, o)
        for s, a in zip(sl, o):
            zh = pz_ref[rows, s].astype(F32)
            o_ref[rows, s] = (a * (zh * jax.nn.sigmoid(zh))).astype(o_ref.dtype)
        return carry

    lax.fori_loop(0, pq_ref.shape[0] // C, chunk, 0)


def _gdn_core(p3, conv_w, gcol, grow, norm_w, hb, chunks_per_step):
    b, s, w4 = p3.shape
    assert p3.dtype == BF16
    nck = min(chunks_per_step, s // CHUNK)
    tc = nck * CHUNK
    assert s % tc == 0
    v_w = w4 // 4
    heads = v_w // HEAD_W
    groups = heads // hb
    wb = hb * HEAD_W

    def pspec(part):
        return pl.BlockSpec((None, tc, wb), lambda bi, gi, ci, part=part: (bi, ci, part * groups + gi))

    def cspec(part):
        return pl.BlockSpec((CONV_TAPS, wb), lambda bi, gi, ci, part=part: (0, part * groups + gi))

    halo = pltpu.VMEM((2 * CHUNK, wb), BF16)
    return pl.pallas_call(
        functools.partial(_gdn_kernel, hb=hb),
        grid=(b, groups, s // tc),
        in_specs=[pspec(0), pspec(1), pspec(2), pspec(3), cspec(0), cspec(1), cspec(2),
                  pl.BlockSpec((None, None, tc, 2 * hb), lambda bi, gi, ci: (bi, gi, ci, 0)),
                  pl.BlockSpec((None, None, nck, 2 * hb, CHUNK), lambda bi, gi, ci: (bi, gi, ci, 0, 0)),
                  pl.BlockSpec((1, HEAD_W), lambda bi, gi, ci: (0, 0))],
        out_specs=pl.BlockSpec((None, tc, wb), lambda bi, gi, ci: (bi, ci, gi)),
        out_shape=jax.ShapeDtypeStruct((b, s, v_w), BF16),
        scratch_shapes=[halo, halo, halo, pltpu.VMEM((hb, HEAD_W, HEAD_W), F32)],
        compiler_params=_cparams(("parallel", "parallel", "arbitrary")),
        name="gdn_core",
    )(p3, p3, p3, p3, conv_w, conv_w, conv_w, gcol, grow, norm_w)


def _layer_norm_rows(y, g, b):
    mu = jnp.mean(y, axis=-1, keepdims=True)
    yc = y - mu
    var = jnp.mean(yc * yc, axis=-1, keepdims=True)
    return yc * lax.rsqrt(var + LN_EPS) * g + b


def _row_chunks(rows):
    n = LN_ROW_CHUNKS if rows % (LN_ROW_CHUNKS * V7X_SUBLANES) == 0 else 1
    step = rows // n
    return [slice(k * step, (k + 1) * step) for k in range(n)]


def _matmul_then_ln(chunks, matmul_rows, ln_store_rows):
    pending = None
    for rs in chunks:
        h = matmul_rows(rs)
        if pending is not None:
            ln_store_rows(*pending)
        pending = (rs, h)
    ln_store_rows(*pending)


def _proj_ln_kernel(a_ref, w_ref, x_ref, g_ref, b_ref, o_ref, *, alpha):
    def matmul_rows(rs):
        return jnp.dot(a_ref[rs, :].astype(BF16), w_ref[...], preferred_element_type=F32)

    def ln_store_rows(rs, h):
        o_ref[rs, :] = _layer_norm_rows(alpha * x_ref[rs, :] + h, g_ref[...], b_ref[...])

    _matmul_then_ln(_row_chunks(a_ref.shape[0]), matmul_rows, ln_store_rows)


def _proj_ln(a, w, layer, x2, g, b, alpha, tm):
    t, k = a.shape
    d = w.shape[2]
    tm = min(tm, t)
    vec = pl.BlockSpec((1, d), lambda i: (0, 0))
    return pl.pallas_call(
        functools.partial(_proj_ln_kernel, alpha=alpha),
        grid=(t // tm,),
        in_specs=[pl.BlockSpec((tm, k), lambda i: (i, 0)),
                  pl.BlockSpec((None, k, d), lambda i: (layer, 0, 0)),
                  pl.BlockSpec((tm, d), lambda i: (i, 0)), vec, vec],
        out_specs=pl.BlockSpec((tm, d), lambda i: (i, 0)),
        out_shape=jax.ShapeDtypeStruct((t, d), F32),
        compiler_params=_cparams(("parallel",)),
        name="proj_ln",
    )(a, w, x2, g, b)


def _mlp_ln_kernel(x_ref, wu_ref, wd_ref, g_ref, b_ref, o_ref, xb_ref, acc_ref, *, alpha):
    f = pl.program_id(1)

    @pl.when(f == 0)
    def _():
        xb_ref[...] = x_ref[...].astype(BF16)
        acc_ref[...] = jnp.zeros_like(acc_ref)

    h = jnp.dot(xb_ref[...], wu_ref[...], preferred_element_type=F32)
    h = jnp.maximum(h, 0.0)
    h = h * h
    acc_ref[...] += jnp.dot(h.astype(BF16), wd_ref[...], preferred_element_type=F32)

    @pl.when(f == pl.num_programs(1) - 1)
    def _():
        o_ref[...] = _layer_norm_rows(alpha * x_ref[...] + acc_ref[...], g_ref[...], b_ref[...])


def _mlp_ln(x2, w_up, w_down, layer, g, b, alpha, tm, tf):
    t, d = x2.shape
    ff = w_up.shape[2]
    tm = min(tm, t)
    tf = min(tf, ff)
    vec = pl.BlockSpec((1, d), lambda i, f: (0, 0))
    return pl.pallas_call(
        functools.partial(_mlp_ln_kernel, alpha=alpha),
        grid=(t // tm, ff // tf),
        in_specs=[pl.BlockSpec((tm, d), lambda i, f: (i, 0)),
                  pl.BlockSpec((None, d, tf), lambda i, f: (layer, 0, f)),
                  pl.BlockSpec((None, tf, d), lambda i, f: (layer, f, 0)), vec, vec],
        out_specs=pl.BlockSpec((tm, d), lambda i, f: (i, 0)),
        out_shape=jax.ShapeDtypeStruct((t, d), F32),
        scratch_shapes=[pltpu.VMEM((tm, d), BF16), pltpu.VMEM((tm, d), F32)],
        compiler_params=_cparams(("parallel", "arbitrary")),
        name="mlp_ln",
    )(x2, w_up, w_down, g, b)


def _diff_attn_kernel(q_ref, k_ref, v_ref, lam_ref, sw_ref, o_ref, qs_ref, s_ref, acc_ref,
                      *, tq, tk, nh, lambda_init):
    nq = q_ref.shape[0] // tq
    heads = list(range(nh))
    hs = [slice(h * HEAD_W, (h + 1) * HEAD_W) for h in heads]
    half = HEAD_W // 2
    lane = lax.broadcasted_iota(jnp.int32, (tq, HEAD_W), 1)
    cw = min(V7X_MXU_COLS, tq)
    assert tq % cw == 0
    ones_rows = jnp.ones((ONES_ROWS, tk), BF16)

    def stack_queries(i):
        qrows = pl.ds(pl.multiple_of(i * tq, tq), tq)
        for h in heads:
            q = q_ref[qrows, hs[h]]
            zero = jnp.zeros_like(q)
            qs_ref[h, 0:tq, :] = jnp.where(lane < half, q, zero)
            qs_ref[h, tq:2 * tq, :] = jnp.where(lane >= half, q, zero)

    def produce(slot, j, diag=None):
        rows = pl.ds(pl.multiple_of(j * tk, tk), tk)
        first = 0 if diag is None else diag * tk
        for h in heads:
            kb = k_ref[rows, hs[h]]
            for lo, hi in ([(0, 2 * tq)] if first == 0 else [(first, tq), (tq + first, 2 * tq)]):
                s_ref[h, slot, :, lo:hi] = lax.dot_general(kb, qs_ref[h, lo:hi, :], (((1,), (1,)), ((), ())),
                                                           preferred_element_type=F32)

    def consume(slot, j, m_prev, diag):
        rows = pl.ds(pl.multiple_of(j * tk, tk), tk)
        lhs = [jnp.concatenate([v_ref[rows, hs[h]].T, ones_rows], axis=0) for h in heads]
        m_parts = [[] for _ in heads]
        for c in range(2 * tq // cw):
            cs = slice(c * cw, (c + 1) * cw)
            r_min = (c * cw) % tq
            if diag is not None and diag * tk > r_min + cw - 1:
                for h in heads:
                    m_parts[h].append(m_prev[h][:, cs])
                continue
            nk = tk if diag is None else min(tk, r_min + cw - diag * tk)
            for h in heads:
                m_prev_c = m_prev[h][:, cs]
                sc = s_ref[h, slot, 0:nk, cs]
                if diag is not None and diag * tk + nk - 1 > r_min:
                    key = lax.broadcasted_iota(jnp.int32, (nk, cw), 0) + diag * tk
                    qry = lax.broadcasted_iota(jnp.int32, (nk, cw), 1) + r_min
                    sc = jnp.where(key <= qry, sc, -jnp.inf)
                m_new_c = jnp.maximum(m_prev_c, jnp.max(sc, axis=0, keepdims=True))
                alpha_c = jnp.exp2(m_prev_c - m_new_c)
                p_c = jnp.exp2(sc - m_new_c).astype(BF16)
                pv_c = jnp.dot(lhs[h][:, 0:nk], p_c, preferred_element_type=F32)
                acc_ref[h, :, cs] = alpha_c * acc_ref[h, :, cs] + pv_c
                m_parts[h].append(m_new_c)
        return tuple(jnp.concatenate(parts, axis=1) for parts in m_parts)

    assert tq == 2 * tk

    def pair(jj, m):
        produce(1, 2 * jj + 1)
        m = consume(0, 2 * jj, m, None)
        produce(0, 2 * jj + 2)
        return consume(1, 2 * jj + 1, m, None)

    lp = lam_ref[...]
    lam = (jnp.exp(jnp.sum(lp[0:1, :] * lp[1:2, :], axis=-1, keepdims=True))
           - jnp.exp(jnp.sum(lp[2:3, :] * lp[3:4, :], axis=-1, keepdims=True)) + lambda_init)

    def finalize(i):
        qrows = pl.ds(pl.multiple_of(i * tq, tq), tq)
        o_all = [acc_ref[h, 0:HEAD_W, :] * (1.0 / acc_ref[h, HEAD_W:HEAD_W + 1, :]) for h in heads]
        o = [(a[:, 0:tq] - lam * a[:, tq:2 * tq]).T for a in o_all]
        o = [a * lax.rsqrt(jnp.mean(a * a, axis=-1, keepdims=True) + SUBLN_EPS) * sw_ref[...] for a in o]
        for h in heads:
            o_ref[qrows, hs[h]] = (o[h] * (1.0 - lambda_init)).astype(o_ref.dtype)

    def attend(i):
        m = lax.fori_loop(0, i, pair, tuple(jnp.full((1, 2 * tq), -jnp.inf, F32) for _ in heads))
        produce(1, 2 * i + 1, 1)
        m = consume(0, 2 * i, m, 0)
        consume(1, 2 * i + 1, m, 1)

    acc_ref[...] = jnp.ones_like(acc_ref)

    def query_block(i, carry):
        stack_queries(i)
        produce(0, 0)
        finalize(jnp.maximum(i - 1, 0))
        acc_ref[...] = jnp.zeros_like(acc_ref)
        attend(i)
        return carry

    lax.fori_loop(0, nq, query_block, 0)
    finalize(nq - 1)


def _diff_attention(q3, kv3, lam_params, subln_w, lambda_init, tq, tk, nh):
    b, s, wq = q3.shape
    heads = wq // HEAD_W
    tq = min(tq, s)
    tk = min(tk, tq)
    nh = min(nh, heads)
    groups = heads // nh
    wb = nh * HEAD_W
    return pl.pallas_call(
        functools.partial(_diff_attn_kernel, tq=tq, tk=tk, nh=nh, lambda_init=lambda_init),
        grid=(b, groups),
        in_specs=[pl.BlockSpec((None, s, wb), lambda bi, g: (bi, 0, g)),
                  pl.BlockSpec((None, s, wb), lambda bi, g: (bi, 0, g)),
                  pl.BlockSpec((None, s, wb), lambda bi, g, groups=groups: (bi, 0, groups + g)),
                  pl.BlockSpec(lam_params.shape, lambda bi, g: (0, 0)),
                  pl.BlockSpec((1, HEAD_W), lambda bi, g: (0, 0))],
        out_specs=pl.BlockSpec((None, s, wb), lambda bi, g: (bi, 0, g)),
        out_shape=jax.ShapeDtypeStruct((b, s, wq), BF16),
        scratch_shapes=[pltpu.VMEM((nh, 2 * tq, HEAD_W), BF16),
                        pltpu.VMEM((nh, 2, tk, 2 * tq), F32),
                        pltpu.VMEM((nh, HEAD_W + ONES_ROWS, 2 * tq), F32)],
        compiler_params=_cparams(("parallel", "parallel")),
        name="diff_attn",
    )(q3, kv3, kv3, lam_params, subln_w)


def _gdn_layer(x2, bsz, seq, layer, w_in_f32, w_in_b16, conv_w, a_log, dt_bias, norm_w, w_out_b16, ln_g, ln_b,
               alpha, hb, tiles):
    t, d = x2.shape
    heads = a_log.shape[0]
    v_w = heads * HEAD_W
    groups = heads // hb
    main = 4 * v_w
    p = _matmul(x2, w_in_b16, layer, BF16, tiles["tm"], tiles["tn"], name="gdn_in_proj")
    wb = w_in_f32[layer, :, main:main + heads].reshape(d, groups, hb)
    wa = w_in_f32[layer, :, main + heads:main + 2 * heads].reshape(d, groups, hb)
    w_gate = jnp.concatenate([wb, wa], axis=2).reshape(d, 2 * heads)
    w_gate = jnp.pad(w_gate, ((0, 0), (0, GATE_W - 2 * heads)))

    def per_col(vals, fill):
        zeros = jnp.full((groups, hb), fill, F32)
        cols = jnp.concatenate([zeros, vals.astype(F32).reshape(groups, hb)], axis=1).reshape(1, 2 * heads)
        return jnp.pad(cols, ((0, 0), (0, GATE_W - 2 * heads)), constant_values=fill)

    is_a = per_col(jnp.ones((heads,), F32), 0.0)
    neg_a = per_col(-jnp.exp(a_log.astype(F32)), 0.0)
    dtb = per_col(dt_bias, 0.0)
    gates = _gates(x2, w_gate, is_a, neg_a, dtb, tiles["tm_gate"])[:, :2 * heads]
    gcol = gates.reshape(bsz, seq, groups, 2 * hb).transpose(0, 2, 1, 3)
    grow = gates.reshape(bsz, seq // CHUNK, CHUNK, groups, 2 * hb).transpose(0, 3, 1, 4, 2)
    o = _gdn_core(p.reshape(bsz, seq, main), conv_w, gcol, grow, norm_w.reshape(1, HEAD_W), hb,
                  tiles["gdn_chunks"])
    return _proj_ln(o.reshape(t, v_w), w_out_b16, layer, x2, ln_g.reshape(1, d), ln_b.reshape(1, d),
                    alpha, tiles["tm_proj"])


def _tiles():
    return dict(tm=1024, tn=2048, tm_gate=512, tm_proj=512, tm_mlp=512, tf_mlp=1024, tq=1024, tk=512, nh=2, hb=16,
                gdn_chunks=8)


def _forward(x, gdn_w_in, gdn_conv_w, gdn_a_log, gdn_dt_bias, gdn_norm_w, gdn_w_out, diff_w_q, diff_lambda,
             diff_subln_w, diff_w_o, shared_w_kv, mlp_w_up, mlp_w_down, ln_g, ln_b, tiles):
    bsz, seq, d = x.shape
    t = bsz * seq
    depth = mlp_w_up.shape[0]
    n_a = gdn_w_in.shape[0]
    alpha = (2 * depth) ** 0.25
    x2 = x.reshape(t, d).astype(F32)
    w_in_b16 = gdn_w_in[:, :, :4 * gdn_a_log.shape[1] * HEAD_W].astype(BF16)
    w_out_b16 = gdn_w_out.astype(BF16)
    w_q_b16 = diff_w_q.astype(BF16)
    w_o_b16 = diff_w_o.astype(BF16)
    w_kv_b16 = shared_w_kv.astype(BF16)[None]
    w_up_b16 = mlp_w_up.astype(BF16)
    w_down_b16 = mlp_w_down.astype(BF16)
    kv3 = None
    for l in range(depth):
        if l < n_a:
            x2 = _gdn_layer(x2, bsz, seq, l, gdn_w_in, w_in_b16, gdn_conv_w[l], gdn_a_log[l], gdn_dt_bias[l],
                            gdn_norm_w[l], w_out_b16, ln_g[l, 0], ln_b[l, 0], alpha,
                            min(tiles["hb"], gdn_a_log.shape[1]), tiles)
        else:
            j = l - n_a
            lambda_init = 0.8 - 0.6 * math.exp(-0.3 * l)
            dk = diff_lambda.shape[-1]
            q = _matmul(x2, w_q_b16, j, BF16, tiles["tm"], tiles["tn"],
                        scale=dk ** -0.5 * math.log2(math.e), name="diff_q_proj")
            o = _diff_attention(q.reshape(bsz, seq, -1), kv3, diff_lambda[j].astype(F32),
                                diff_subln_w[j].reshape(1, HEAD_W).astype(F32), lambda_init, tiles["tq"],
                                tiles["tk"], tiles["nh"])
            x2 = _proj_ln(o.reshape(t, -1), w_o_b16, j, x2, ln_g[l, 0].reshape(1, d),
                          ln_b[l, 0].reshape(1, d), alpha, tiles["tm_proj"])
        x2 = _mlp_ln(x2, w_up_b16, w_down_b16, l, ln_g[l, 1].reshape(1, d),
                     ln_b[l, 1].reshape(1, d), alpha, tiles["tm_mlp"], tiles["tf_mlp"])
        if l == n_a - 1:
            kv = _matmul(x2, w_kv_b16, 0, BF16, tiles["tm"], tiles["tn"], name="shared_kv_proj")
            kv3 = kv.reshape(bsz, seq, -1)
    return x2.reshape(bsz, seq, d).astype(x.dtype)


def kernel(x, gdn_w_in, gdn_conv_w, gdn_a_log, gdn_dt_bias, gdn_norm_w, gdn_w_out, diff_w_q, diff_lambda,
           diff_subln_w, diff_w_o, shared_w_kv, mlp_w_up, mlp_w_down, ln_g, ln_b):
    return _forward(x, gdn_w_in, gdn_conv_w, gdn_a_log, gdn_dt_bias, gdn_norm_w, gdn_w_out, diff_w_q,
                    diff_lambda, diff_subln_w, diff_w_o, shared_w_kv, mlp_w_up, mlp_w_down, ln_g, ln_b,
                    _tiles())
```

```python
import functools
import math

import jax
import jax.numpy as jnp
from jax import lax
from jax.experimental import pallas as pl
from jax.experimental.pallas import tpu as pltpu

F32 = jnp.float32
BF16 = jnp.bfloat16

HEAD_W = 128
CHUNK = 64
CONV_TAPS = 4
GDN_EPS = 1e-6
SUBLN_EPS = 1e-5
LN_EPS = 1e-5
GATE_W = 128
LN_ROW_CHUNKS = 4

V7X_SUBLANES = 8
V7X_MXU_COLS = 256
ONES_ROWS = 16
V7X_VMEM_LIMIT_BYTES = 56 * 1024 * 1024


def _cparams(sem):
    return pltpu.CompilerParams(dimension_semantics=sem, vmem_limit_bytes=V7X_VMEM_LIMIT_BYTES)


def _bdot(a, b):
    return jnp.dot(a.astype(BF16), b.astype(BF16), preferred_element_type=F32)


def _bdot_nt(a, b):
    return lax.dot_general(a.astype(BF16), b.astype(BF16), (((1,), (1,)), ((), ())),
                           preferred_element_type=F32)


def _bdot_tn(a, b):
    return lax.dot_general(a.astype(BF16), b.astype(BF16), (((0,), (0,)), ((), ())),
                           preferred_element_type=F32)


def _split3(x):
    h = x.astype(BF16)
    r = x - h.astype(F32)
    m = r.astype(BF16)
    l = (r - m.astype(F32)).astype(BF16)
    return h, m, l


def _mm_kernel(x_ref, w_ref, o_ref, *, scale):
    acc = jnp.dot(x_ref[...].astype(BF16), w_ref[...], preferred_element_type=F32)
    if scale != 1.0:
        acc = acc * scale
    o_ref[...] = acc.astype(o_ref.dtype)


def _matmul(x, w, layer, out_dtype, tm, tn, scale=1.0, name="matmul"):
    m, k = x.shape
    n = w.shape[2]
    tm = min(tm, m)
    tn = min(tn, n)
    assert m % tm == 0 and n % tn == 0
    return pl.pallas_call(
        functools.partial(_mm_kernel, scale=scale),
        grid=(m // tm, n // tn),
        in_specs=[pl.BlockSpec((tm, k), lambda i, j: (i, 0)),
                  pl.BlockSpec((None, k, tn), lambda i, j: (layer, 0, j))],
        out_specs=pl.BlockSpec((tm, tn), lambda i, j: (i, j)),
        out_shape=jax.ShapeDtypeStruct((m, n), out_dtype),
        compiler_params=_cparams(("parallel", "parallel")),
        name=name,
    )(x, w)


def _in_proj_kernel(x_ref, w_ref, wg_ref, isa_ref, nega_ref, dtb_ref, p_ref, g_ref):
    p_ref[...] = jnp.dot(x_ref[...].astype(BF16), w_ref[...], preferred_element_type=F32).astype(p_ref.dtype)

    @pl.when(pl.program_id(1) == 0)
    def _():
        acc = jnp.dot(x_ref[...].astype(BF16), wg_ref[...], preferred_element_type=F32)
        beta = jax.nn.sigmoid(acc)
        s = acc + dtb_ref[...]
        softplus = jnp.maximum(s, 0.0) + jnp.log1p(jnp.exp(-jnp.abs(s)))
        g = nega_ref[...] * softplus
        g_ref[...] = jnp.where(isa_ref[...] > 0.5, g, beta)


def _in_proj(x2, w, layer, w_gate, is_a, neg_a, dtb, tm, tn):
    t, d = x2.shape
    n = w.shape[2]
    tm = min(tm, t)
    tn = min(tn, n)
    assert t % tm == 0 and n % tn == 0
    vec = pl.BlockSpec((1, GATE_W), lambda i, j: (0, 0))
    return pl.pallas_call(
        _in_proj_kernel,
        grid=(t // tm, n // tn),
        in_specs=[pl.BlockSpec((tm, d), lambda i, j: (i, 0)),
                  pl.BlockSpec((None, d, tn), lambda i, j: (layer, 0, j)),
                  pl.BlockSpec((d, GATE_W), lambda i, j: (0, 0)), vec, vec, vec],
        out_specs=[pl.BlockSpec((tm, tn), lambda i, j: (i, j)),
                   pl.BlockSpec((tm, GATE_W), lambda i, j: (i, 0))],
        out_shape=[jax.ShapeDtypeStruct((t, n), BF16), jax.ShapeDtypeStruct((t, GATE_W), F32)],
        compiler_params=_cparams(("parallel", "arbitrary")),
        name="gdn_in_proj",
    )(x2, w, w_gate, is_a, neg_a, dtb)


def _each(fn, *lists):
    return [fn(*args) for args in zip(*lists)]


def _unit_lower_inverse_minus_identity(lows, same16, same32):
    c = lows[0].shape[0]
    d = _each(lambda low: jnp.where(same16, low, 0.0), lows)
    x = _each(lambda a: -a, d)
    p = _each(lambda a: _bdot(a, a), d)
    xp = _each(lambda a, b: _bdot(jnp.concatenate([a, b], axis=0), b), x, p)
    n = _each(lambda a, b, ab: a + b + ab[:c], x, p, xp)
    p2 = _each(lambda ab: ab[c:], xp)
    np2 = _each(lambda a, b: _bdot(jnp.concatenate([a, b], axis=0), b), n, p2)
    n = _each(lambda a, b, ab: a + b + ab[:c], n, p2, np2)
    p4 = _each(lambda ab: ab[c:], np2)
    n = _each(lambda a, b: a + b + _bdot(a, b), n, p4)
    only32 = jnp.logical_and(same32, jnp.logical_not(same16))
    off1 = _each(lambda low: jnp.where(only32, low, 0.0), lows)
    c1 = _each(lambda a, o: o + _bdot(a, o), n, off1)
    n = _each(lambda a, cc: a - (cc + _bdot(cc, a)), n, c1)
    off2 = _each(lambda low: jnp.where(same32, 0.0, low), lows)
    c2 = _each(lambda a, o: o + _bdot(a, o), n, off2)
    n = _each(lambda a, cc: a - (cc + _bdot(cc, a)), n, c2)
    return n


def _gdn_kernel(pq_ref, pk_ref, pv_ref, pz_ref, cwq_ref, cwk_ref, cwv_ref, gcol_ref, grow_ref,
                nw_ref, o_ref, hq_ref, hk_ref, hv_ref, state_ref, *, hb):
    c_idx = pl.program_id(2)
    C = CHUNK
    w_blk = hb * HEAD_W

    @pl.when(c_idx == 0)
    def _():
        state_ref[...] = jnp.zeros_like(state_ref)
        zero = jnp.zeros((C, w_blk), BF16)
        hq_ref[0:C, :] = zero
        hk_ref[0:C, :] = zero
        hv_ref[0:C, :] = zero

    sr = lax.broadcasted_iota(jnp.int32, ((CONV_TAPS - 1) * C, 2 * C), 0)
    sc = lax.broadcasted_iota(jnp.int32, ((CONV_TAPS - 1) * C, 2 * C), 1)
    tap = sr // C
    shift_mat = jnp.where(sc == sr - tap * (C - 1) + (C - (CONV_TAPS - 1)), 1.0, 0.0).astype(BF16)

    def conv_silu(p_ref, h_ref, cw_ref, rows):
        cur = p_ref[rows, :]
        h_ref[C:2 * C, :] = cur
        shifted = jnp.dot(shift_mat, h_ref[...], preferred_element_type=F32)
        w = cw_ref[...]
        y = shifted[0:C] * w[0:1, :]
        for j in range(1, CONV_TAPS - 1):
            y = y + shifted[j * C:(j + 1) * C] * w[j:j + 1, :]
        y = y + cur.astype(F32) * w[CONV_TAPS - 1:CONV_TAPS, :]
        h_ref[0:C, :] = cur
        return y * jax.nn.sigmoid(y)

    row = lax.broadcasted_iota(jnp.int32, (C, C), 0)
    col = lax.broadcasted_iota(jnp.int32, (C, C), 1)
    causal = row >= col
    strict = row > col
    same16 = (row // 16) == (col // 16)
    same32 = (row // 32) == (col // 32)
    tril = jnp.where(causal, 1.0, 0.0).astype(BF16)
    triu = jnp.where(row <= col, 1.0, 0.0).astype(BF16)

    nw = nw_ref[...]
    q_scale = HEAD_W ** -0.5

    def l2n(y, scale):
        return y * (lax.rsqrt(jnp.sum(y * y, axis=-1, keepdims=True) + GDN_EPS) * scale)

    def chunk(cc, carry):
        rows = pl.ds(pl.multiple_of(cc * C, C), C)
        yq = conv_silu(pq_ref, hq_ref, cwq_ref, rows)
        yk = conv_silu(pk_ref, hk_ref, cwk_ref, rows)
        yv = conv_silu(pv_ref, hv_ref, cwv_ref, rows)

        gcol = gcol_ref[rows, :]
        grow = grow_ref[cc]
        gh, gm, gl = _split3(gcol[:, hb:])
        gc_col = (jnp.dot(tril, gh, preferred_element_type=F32)
                  + jnp.dot(tril, gm, preferred_element_type=F32)
                  + jnp.dot(tril, gl, preferred_element_type=F32))
        rh, rm, rl = _split3(grow[hb:, :])
        gc_row = (jnp.dot(rh, triu, preferred_element_type=F32)
                  + jnp.dot(rm, triu, preferred_element_type=F32)
                  + jnp.dot(rl, triu, preferred_element_type=F32))

        heads = list(range(hb))
        sl = [slice(h * HEAD_W, (h + 1) * HEAD_W) for h in heads]
        q = [l2n(yq[:, s], q_scale) for s in sl]
        k = [l2n(yk[:, s], 1.0) for s in sl]
        v = [yv[:, s] for s in sl]
        beta = [gcol[:, h:h + 1] for h in heads]
        g_c = [gc_col[:, h:h + 1] for h in heads]
        g_r = [gc_row[h:h + 1, :] for h in heads]
        g_last = [r[:, C - 1:C] for r in g_r]
        decay = _each(lambda gc, gr: jnp.exp(jnp.where(causal, gc - gr, -jnp.inf)), g_c, g_r)
        eg = _each(jnp.exp, g_c)
        k_beta = _each(lambda a, b: a * b, k, beta)
        kq = _each(lambda kb, qq, kk: _bdot_nt(jnp.concatenate([kb, qq], axis=0), kk), k_beta, q, k)
        low = _each(lambda a, dc: jnp.where(strict, a[:C] * dc, 0.0), kq, decay)
        attn = _each(lambda a, dc: a[C:] * dc, kq, decay)
        n_inv = _unit_lower_inverse_minus_identity(low, same16, same32)
        rhs = _each(lambda vv, b, kb, e: jnp.concatenate([vv * b, kb * e], axis=1), v, beta, k_beta, eg)
        sol = _each(lambda r, n: r + _bdot(n, r), rhs, n_inv)
        state = [state_ref[h] for h in heads]
        ws = _each(lambda s, qq, e, st: _bdot(jnp.concatenate([s[:, HEAD_W:], qq * e], axis=0), st),
                   sol, q, eg, state)
        v_new = _each(lambda s, a: s[:, :HEAD_W] - a[:C], sol, ws)
        o = _each(lambda a, at, vn: a[C:] + _bdot(at, vn), ws, attn, v_new)
        k_dec = _each(lambda kk, gl_, gc: kk * jnp.exp(gl_ - gc), k, g_last, g_c)
        new_state = _each(lambda st, gl_, kd, vn: st * jnp.exp(gl_) + _bdot_tn(kd, vn),
                          state, g_last, k_dec, v_new)
        for h, st in zip(heads, new_state):
            state_ref[h] = st
        o = _each(lambda a: a * lax.rsqrt(jnp.mean(a * a, axis=-1, keepdims=True) + GDN_EPS) * nw, o)
        for s, a in zip(sl, o):
            zh = pz_ref[rows, s].astype(F32)
            o_ref[rows, s] = (a * (zh * jax.nn.sigmoid(zh))).astype(o_ref.dtype)
        return carry

    lax.fori_loop(0, pq_ref.shape[0] // C, chunk, 0)


def _gdn_core(p3, conv_w, gcol, grow, norm_w, hb, chunks_per_step):
    b, s, w4 = p3.shape
    assert p3.dtype == BF16
    nck = min(chunks_per_step, s // CHUNK)
    tc = nck * CHUNK
    assert s % tc == 0
    v_w = w4 // 4
    heads = v_w // HEAD_W
    groups = heads // hb
    wb = hb * HEAD_W

    def pspec(part):
        return pl.BlockSpec((None, tc, wb), lambda bi, gi, ci, part=part: (bi, ci, part * groups + gi))

    def cspec(part):
        return pl.BlockSpec((CONV_TAPS, wb), lambda bi, gi, ci, part=part: (0, part * groups + gi))

    halo = pltpu.VMEM((2 * CHUNK, wb), BF16)
    return pl.pallas_call(
        functools.partial(_gdn_kernel, hb=hb),
        grid=(b, groups, s // tc),
        in_specs=[pspec(0), pspec(1), pspec(2), pspec(3), cspec(0), cspec(1), cspec(2),
                  pl.BlockSpec((None, None, tc, 2 * hb), lambda bi, gi, ci: (bi, gi, ci, 0)),
                  pl.BlockSpec((None, None, nck, 2 * hb, CHUNK), lambda bi, gi, ci: (bi, gi, ci, 0, 0)),
                  pl.BlockSpec((1, HEAD_W), lambda bi, gi, ci: (0, 0))],
        out_specs=pl.BlockSpec((None, tc, wb), lambda bi, gi, ci: (bi, ci, gi)),
        out_shape=jax.ShapeDtypeStruct((b, s, v_w), BF16),
        scratch_shapes=[halo, halo, halo, pltpu.VMEM((hb, HEAD_W, HEAD_W), F32)],
        compiler_params=_cparams(("parallel", "parallel", "arbitrary")),
        name="gdn_core",
    )(p3, p3, p3, p3, conv_w, conv_w, conv_w, gcol, grow, norm_w)


def _layer_norm_rows(y, g, b):
    mu = jnp.mean(y, axis=-1, keepdims=True)
    yc = y - mu
    var = jnp.mean(yc * yc, axis=-1, keepdims=True)
    return yc * lax.rsqrt(var + LN_EPS) * g + b


def _row_chunks(rows):
    n = LN_ROW_CHUNKS if rows % (LN_ROW_CHUNKS * V7X_SUBLANES) == 0 else 1
    step = rows // n
    return [slice(k * step, (k + 1) * step) for k in range(n)]


def _matmul_then_ln(chunks, matmul_rows, ln_store_rows):
    pending = None
    for rs in chunks:
        h = matmul_rows(rs)
        if pending is not None:
            ln_store_rows(*pending)
        pending = (rs, h)
    ln_store_rows(*pending)


def _proj_ln_kernel(a_ref, w_ref, x_ref, g_ref, b_ref, o_ref, *, alpha):
    def matmul_rows(rs):
        return jnp.dot(a_ref[rs, :].astype(BF16), w_ref[...], preferred_element_type=F32)

    def ln_store_rows(rs, h):
        o_ref[rs, :] = _layer_norm_rows(alpha * x_ref[rs, :] + h, g_ref[...], b_ref[...])

    _matmul_then_ln(_row_chunks(a_ref.shape[0]), matmul_rows, ln_store_rows)


def _proj_ln(a, w, layer, x2, g, b, alpha, tm):
    t, k = a.shape
    d = w.shape[2]
    tm = min(tm, t)
    vec = pl.BlockSpec((1, d), lambda i: (0, 0))
    return pl.pallas_call(
        functools.partial(_proj_ln_kernel, alpha=alpha),
        grid=(t // tm,),
        in_specs=[pl.BlockSpec((tm, k), lambda i: (i, 0)),
                  pl.BlockSpec((None, k, d), lambda i: (layer, 0, 0)),
                  pl.BlockSpec((tm, d), lambda i: (i, 0)), vec, vec],
        out_specs=pl.BlockSpec((tm, d), lambda i: (i, 0)),
        out_shape=jax.ShapeDtypeStruct((t, d), F32),
        compiler_params=_cparams(("parallel",)),
        name="proj_ln",
    )(a, w, x2, g, b)


def _mlp_ln_kernel(x_ref, wu_ref, wd_ref, g_ref, b_ref, o_ref, xb_ref, acc_ref, *, alpha):
    f = pl.program_id(1)

    @pl.when(f == 0)
    def _():
        xb_ref[...] = x_ref[...].astype(BF16)
        acc_ref[...] = jnp.zeros_like(acc_ref)

    h = jnp.dot(xb_ref[...], wu_ref[...], preferred_element_type=F32)
    h = jnp.maximum(h, 0.0)
    h = h * h
    acc_ref[...] += jnp.dot(h.astype(BF16), wd_ref[...], preferred_element_type=F32)

    @pl.when(f == pl.num_programs(1) - 1)
    def _():
        o_ref[...] = _layer_norm_rows(alpha * x_ref[...] + acc_ref[...], g_ref[...], b_ref[...])


def _mlp_ln(x2, w_up, w_down, layer, g, b, alpha, tm, tf):
    t, d = x2.shape
    ff = w_up.shape[2]
    tm = min(tm, t)
    tf = min(tf, ff)
    vec = pl.BlockSpec((1, d), lambda i, f: (0, 0))
    return pl.pallas_call(
        functools.partial(_mlp_ln_kernel, alpha=alpha),
        grid=(t // tm, ff // tf),
        in_specs=[pl.BlockSpec((tm, d), lambda i, f: (i, 0)),
                  pl.BlockSpec((None, d, tf), lambda i, f: (layer, 0, f)),
                  pl.BlockSpec((None, tf, d), lambda i, f: (layer, f, 0)), vec, vec],
        out_specs=pl.BlockSpec((tm, d), lambda i, f: (i, 0)),
        out_shape=jax.ShapeDtypeStruct((t, d), F32),
        scratch_shapes=[pltpu.VMEM((tm, d), BF16), pltpu.VMEM((tm, d), F32)],
        compiler_params=_cparams(("parallel", "arbitrary")),
        name="mlp_ln",
    )(x2, w_up, w_down, g, b)


def _diff_attn_kernel(q_ref, k_ref, v_ref, lam_ref, sw_ref, o_ref, qs_ref, s_ref, acc_ref,
                      *, tq, tk, nh, lambda_init):
    nq = q_ref.shape[0] // tq
    heads = list(range(nh))
    hs = [slice(h * HEAD_W, (h + 1) * HEAD_W) for h in heads]
    half = HEAD_W // 2
    lane = lax.broadcasted_iota(jnp.int32, (tq, HEAD_W), 1)
    cw = min(V7X_MXU_COLS, tq)
    assert tq % cw == 0
    ones_rows = jnp.ones((ONES_ROWS, tk), BF16)

    def stack_queries(i):
        qrows = pl.ds(pl.multiple_of(i * tq, tq), tq)
        for h in heads:
            q = q_ref[qrows, hs[h]]
            zero = jnp.zeros_like(q)
            qs_ref[h, 0:tq, :] = jnp.where(lane < half, q, zero)
            qs_ref[h, tq:2 * tq, :] = jnp.where(lane >= half, q, zero)

    def produce(slot, j, diag=None):
        rows = pl.ds(pl.multiple_of(j * tk, tk), tk)
        first = 0 if diag is None else diag * tk
        for h in heads:
            kb = k_ref[rows, hs[h]]
            for lo, hi in ([(0, 2 * tq)] if first == 0 else [(first, tq), (tq + first, 2 * tq)]):
                s_ref[h, slot, :, lo:hi] = lax.dot_general(kb, qs_ref[h, lo:hi, :], (((1,), (1,)), ((), ())),
                                                           preferred_element_type=F32)

    def consume(slot, j, m_prev, diag):
        rows = pl.ds(pl.multiple_of(j * tk, tk), tk)
        lhs = [jnp.concatenate([v_ref[rows, hs[h]].T, ones_rows], axis=0) for h in heads]
        m_parts = [[] for _ in heads]
        for c in range(2 * tq // cw):
            cs = slice(c * cw, (c + 1) * cw)
            r_min = (c * cw) % tq
            if diag is not None and diag * tk > r_min + cw - 1:
                for h in heads:
                    m_parts[h].append(m_prev[h][:, cs])
                continue
            nk = tk if diag is None else min(tk, r_min + cw - diag * tk)
            for h in heads:
                m_prev_c = m_prev[h][:, cs]
                sc = s_ref[h, slot, 0:nk, cs]
                if diag is not None and diag * tk + nk - 1 > r_min:
                    key = lax.broadcasted_iota(jnp.int32, (nk, cw), 0) + diag * tk
                    qry = lax.broadcasted_iota(jnp.int32, (nk, cw), 1) + r_min
                    sc = jnp.where(key <= qry, sc, -jnp.inf)
                m_new_c = jnp.maximum(m_prev_c, jnp.max(sc, axis=0, keepdims=True))
                alpha_c = jnp.exp2(m_prev_c - m_new_c)
                p_c = jnp.exp2(sc - m_new_c).astype(BF16)
                pv_c = jnp.dot(lhs[h][:, 0:nk], p_c, preferred_element_type=F32)
                acc_ref[h, :, cs] = alpha_c * acc_ref[h, :, cs] + pv_c
                m_parts[h].append(m_new_c)
        return tuple(jnp.concatenate(parts, axis=1) for parts in m_parts)

    assert tq == 2 * tk

    def pair(jj, m):
        produce(1, 2 * jj + 1)
        m = consume(0, 2 * jj, m, None)
        produce(0, 2 * jj + 2)
        return consume(1, 2 * jj + 1, m, None)

    lp = lam_ref[...]
    lam = (jnp.exp(jnp.sum(lp[0:1, :] * lp[1:2, :], axis=-1, keepdims=True))
           - jnp.exp(jnp.sum(lp[2:3, :] * lp[3:4, :], axis=-1, keepdims=True)) + lambda_init)

    def finalize(i):
        qrows = pl.ds(pl.multiple_of(i * tq, tq), tq)
        o_all = [acc_ref[h, 0:HEAD_W, :] * (1.0 / acc_ref[h, HEAD_W:HEAD_W + 1, :]) for h in heads]
        o = [(a[:, 0:tq] - lam * a[:, tq:2 * tq]).T for a in o_all]
        o = [a * lax.rsqrt(jnp.mean(a * a, axis=-1, keepdims=True) + SUBLN_EPS) * sw_ref[...] for a in o]
        for h in heads:
            o_ref[qrows, hs[h]] = (o[h] * (1.0 - lambda_init)).astype(o_ref.dtype)

    def attend(i):
        m = lax.fori_loop(0, i, pair, tuple(jnp.full((1, 2 * tq), -jnp.inf, F32) for _ in heads))
        produce(1, 2 * i + 1, 1)
        m = consume(0, 2 * i, m, 0)
        consume(1, 2 * i + 1, m, 1)

    acc_ref[...] = jnp.ones_like(acc_ref)

    def query_block(i, carry):
        stack_queries(i)
        produce(0, 0)
        finalize(jnp.maximum(i - 1, 0))
        acc_ref[...] = jnp.zeros_like(acc_ref)
        attend(i)
        return carry

    lax.fori_loop(0, nq, query_block, 0)
    finalize(nq - 1)


def _diff_attention(q3, kv3, lam_params, subln_w, lambda_init, tq, tk, nh):
    b, s, wq = q3.shape
    heads = wq // HEAD_W
    tq = min(tq, s)
    tk = min(tk, tq)
    nh = min(nh, heads)
    groups = heads // nh
    wb = nh * HEAD_W
    return pl.pallas_call(
        functools.partial(_diff_attn_kernel, tq=tq, tk=tk, nh=nh, lambda_init=lambda_init),
        grid=(b, groups),
        in_specs=[pl.BlockSpec((None, s, wb), lambda bi, g: (bi, 0, g)),
                  pl.BlockSpec((None, s, wb), lambda bi, g: (bi, 0, g)),
                  pl.BlockSpec((None, s, wb), lambda bi, g, groups=groups: (bi, 0, groups + g)),
                  pl.BlockSpec(lam_params.shape, lambda bi, g: (0, 0)),
                  pl.BlockSpec((1, HEAD_W), lambda bi, g: (0, 0))],
        out_specs=pl.BlockSpec((None, s, wb), lambda bi, g: (bi, 0, g)),
        out_shape=jax.ShapeDtypeStruct((b, s, wq), BF16),
        scratch_shapes=[pltpu.VMEM((nh, 2 * tq, HEAD_W), BF16),
                        pltpu.VMEM((nh, 2, tk, 2 * tq), F32),
                        pltpu.VMEM((nh, HEAD_W + ONES_ROWS, 2 * tq), F32)],
        compiler_params=_cparams(("parallel", "parallel")),
        name="diff_attn",
    )(q3, kv3, kv3, lam_params, subln_w)


def _gdn_layer(x2, bsz, seq, layer, w_in_f32, w_in_b16, conv_w, a_log, dt_bias, norm_w, w_out_b16, ln_g, ln_b,
               alpha, hb, tiles):
    t, d = x2.shape
    heads = a_log.shape[0]
    v_w = heads * HEAD_W
    groups = heads // hb
    main = 4 * v_w
    wb = w_in_f32[layer, :, main:main + heads].reshape(d, groups, hb)
    wa = w_in_f32[layer, :, main + heads:main + 2 * heads].reshape(d, groups, hb)
    w_gate = jnp.concatenate([wb, wa], axis=2).reshape(d, 2 * heads)
    w_gate = jnp.pad(w_gate, ((0, 0), (0, GATE_W - 2 * heads))).astype(BF16)

    def per_col(vals, fill):
        zeros = jnp.full((groups, hb), fill, F32)
        cols = jnp.concatenate([zeros, vals.astype(F32).reshape(groups, hb)], axis=1).reshape(1, 2 * heads)
        return jnp.pad(cols, ((0, 0), (0, GATE_W - 2 * heads)), constant_values=fill)

    is_a = per_col(jnp.ones((heads,), F32), 0.0)
    neg_a = per_col(-jnp.exp(a_log.astype(F32)), 0.0)
    dtb = per_col(dt_bias, 0.0)
    p, gates = _in_proj(x2, w_in_b16, layer, w_gate, is_a, neg_a, dtb, tiles["tm"], tiles["tn"])
    gates = gates[:, :2 * heads]
    gcol = gates.reshape(bsz, seq, groups, 2 * hb).transpose(0, 2, 1, 3)
    grow = gates.reshape(bsz, seq // CHUNK, CHUNK, groups, 2 * hb).transpose(0, 3, 1, 4, 2)
    o = _gdn_core(p.reshape(bsz, seq, main), conv_w, gcol, grow, norm_w.reshape(1, HEAD_W), hb,
                  tiles["gdn_chunks"])
    return _proj_ln(o.reshape(t, v_w), w_out_b16, layer, x2, ln_g.reshape(1, d), ln_b.reshape(1, d),
                    alpha, tiles["tm_proj"])


def _tiles():
    return dict(tm=1024, tn=2048, tm_proj=512, tm_mlp=512, tf_mlp=1024, tq=1024, tk=512, nh=2, hb=16,
                gdn_chunks=8)


def _forward(x, gdn_w_in, gdn_conv_w, gdn_a_log, gdn_dt_bias, gdn_norm_w, gdn_w_out, diff_w_q, diff_lambda,
             diff_subln_w, diff_w_o, shared_w_kv, mlp_w_up, mlp_w_down, ln_g, ln_b, tiles):
    bsz, seq, d = x.shape
    t = bsz * seq
    depth = mlp_w_up.shape[0]
    n_a = gdn_w_in.shape[0]
    alpha = (2 * depth) ** 0.25
    x2 = x.reshape(t, d).astype(F32)
    w_in_b16 = gdn_w_in[:, :, :4 * gdn_a_log.shape[1] * HEAD_W].astype(BF16)
    w_out_b16 = gdn_w_out.astype(BF16)
    w_q_b16 = diff_w_q.astype(BF16)
    w_o_b16 = diff_w_o.astype(BF16)
    w_kv_b16 = shared_w_kv.astype(BF16)[None]
    w_up_b16 = mlp_w_up.astype(BF16)
    w_down_b16 = mlp_w_down.astype(BF16)
    kv3 = None
    for l in range(depth):
        if l < n_a:
            x2 = _gdn_layer(x2, bsz, seq, l, gdn_w_in, w_in_b16, gdn_conv_w[l], gdn_a_log[l], gdn_dt_bias[l],
                            gdn_norm_w[l], w_out_b16, ln_g[l, 0], ln_b[l, 0], alpha,
                            min(tiles["hb"], gdn_a_log.shape[1]), tiles)
        else:
            j = l - n_a
            lambda_init = 0.8 - 0.6 * math.exp(-0.3 * l)
            dk = diff_lambda.shape[-1]
            q = _matmul(x2, w_q_b16, j, BF16, tiles["tm"], tiles["tn"],
                        scale=dk ** -0.5 * math.log2(math.e), name="diff_q_proj")
            o = _diff_attention(q.reshape(bsz, seq, -1), kv3, diff_lambda[j].astype(F32),
                                diff_subln_w[j].reshape(1, HEAD_W).astype(F32), lambda_init, tiles["tq"],
                                tiles["tk"], tiles["nh"])
            x2 = _proj_ln(o.reshape(t, -1), w_o_b16, j, x2, ln_g[l, 0].reshape(1, d),
                          ln_b[l, 0].reshape(1, d), alpha, tiles["tm_proj"])
        x2 = _mlp_ln(x2, w_up_b16, w_down_b16, l, ln_g[l, 1].reshape(1, d),
                     ln_b[l, 1].reshape(1, d), alpha, tiles["tm_mlp"], tiles["tf_mlp"])
        if l == n_a - 1:
            kv = _matmul(x2, w_kv_b16, 0, BF16, tiles["tm"], tiles["tn"], name="shared_kv_proj")
            kv3 = kv.reshape(bsz, seq, -1)
    return x2.reshape(bsz, seq, d).astype(x.dtype)


def kernel(x, gdn_w_in, gdn_conv_w, gdn_a_log, gdn_dt_bias, gdn_norm_w, gdn_w_out, diff_w_q, diff_lambda,
           diff_subln_w, diff_w_o, shared_w_kv, mlp_w_up, mlp_w_down, ln_g, ln_b):
    return _forward(x, gdn_w_in, gdn_conv_w, gdn_a_log, gdn_dt_bias, gdn_norm_w, gdn_w_out, diff_w_q,
                    diff_lambda, diff_subln_w, diff_w_o, shared_w_kv, mlp_w_up, mlp_w_down, ln_g, ln_b,
                    _tiles())
```

```python
import functools
import math

import jax
import jax.numpy as jnp
from jax import lax
from jax.experimental import pallas as pl
from jax.experimental.pallas import tpu as pltpu

F32 = jnp.float32
BF16 = jnp.bfloat16

HEAD_W = 128
CHUNK = 64
CONV_TAPS = 4
GDN_EPS = 1e-6
SUBLN_EPS = 1e-5
LN_EPS = 1e-5
GATE_W = 128
LN_ROW_CHUNKS = 4

V7X_SUBLANES = 8
V7X_MXU_COLS = 256
ONES_ROWS = 16
V7X_VMEM_LIMIT_BYTES = 56 * 1024 * 1024


def _cparams(sem):
    return pltpu.CompilerParams(dimension_semantics=sem, vmem_limit_bytes=V7X_VMEM_LIMIT_BYTES)


def _bdot(a, b):
    return jnp.dot(a.astype(BF16), b.astype(BF16), preferred_element_type=F32)


def _bdot_nt(a, b):
    return lax.dot_general(a.astype(BF16), b.astype(BF16), (((1,), (1,)), ((), ())),
                           preferred_element_type=F32)


def _bdot_tn(a, b):
    return lax.dot_general(a.astype(BF16), b.astype(BF16), (((0,), (0,)), ((), ())),
                           preferred_element_type=F32)


def _split3(x):
    h = x.astype(BF16)
    r = x - h.astype(F32)
    m = r.astype(BF16)
    l = (r - m.astype(F32)).astype(BF16)
    return h, m, l


def _mm_kernel(x_ref, w_ref, o_ref, *, scale):
    acc = jnp.dot(x_ref[...].astype(BF16), w_ref[...], preferred_element_type=F32)
    if scale != 1.0:
        acc = acc * scale
    o_ref[...] = acc.astype(o_ref.dtype)


def _mm_colscale_kernel(x_ref, w_ref, s_ref, o_ref):
    acc = jnp.dot(x_ref[...].astype(BF16), w_ref[...], preferred_element_type=F32)
    o_ref[...] = (acc * s_ref[...]).astype(o_ref.dtype)


def _matmul_colscale(x, w, col_scale, out_dtype, tm, tn, name):
    m, k = x.shape
    n = w.shape[1]
    tm = min(tm, m)
    tn = min(tn, n)
    assert m % tm == 0 and n % tn == 0
    return pl.pallas_call(
        _mm_colscale_kernel,
        grid=(m // tm, n // tn),
        in_specs=[pl.BlockSpec((tm, k), lambda i, j: (i, 0)),
                  pl.BlockSpec((k, tn), lambda i, j: (0, j)),
                  pl.BlockSpec((1, tn), lambda i, j: (0, j))],
        out_specs=pl.BlockSpec((tm, tn), lambda i, j: (i, j)),
        out_shape=jax.ShapeDtypeStruct((m, n), out_dtype),
        compiler_params=_cparams(("parallel", "parallel")),
        name=name,
    )(x, w, col_scale)


def _matmul(x, w, layer, out_dtype, tm, tn, scale=1.0, name="matmul"):
    m, k = x.shape
    n = w.shape[2]
    tm = min(tm, m)
    tn = min(tn, n)
    assert m % tm == 0 and n % tn == 0
    return pl.pallas_call(
        functools.partial(_mm_kernel, scale=scale),
        grid=(m // tm, n // tn),
        in_specs=[pl.BlockSpec((tm, k), lambda i, j: (i, 0)),
                  pl.BlockSpec((None, k, tn), lambda i, j: (layer, 0, j))],
        out_specs=pl.BlockSpec((tm, tn), lambda i, j: (i, j)),
        out_shape=jax.ShapeDtypeStruct((m, n), out_dtype),
        compiler_params=_cparams(("parallel", "parallel")),
        name=name,
    )(x, w)


def _in_proj_kernel(x_ref, w_ref, wg_ref, isa_ref, nega_ref, dtb_ref, p_ref, g_ref):
    p_ref[...] = jnp.dot(x_ref[...].astype(BF16), w_ref[...], preferred_element_type=F32).astype(p_ref.dtype)

    @pl.when(pl.program_id(1) == 0)
    def _():
        acc = jnp.dot(x_ref[...].astype(BF16), wg_ref[...], preferred_element_type=F32)
        beta = jax.nn.sigmoid(acc)
        s = acc + dtb_ref[...]
        softplus = jnp.maximum(s, 0.0) + jnp.log1p(jnp.exp(-jnp.abs(s)))
        g = nega_ref[...] * softplus
        g_ref[...] = jnp.where(isa_ref[...] > 0.5, g, beta)


def _in_proj(x2, w, layer, w_gate, is_a, neg_a, dtb, tm, tn):
    t, d = x2.shape
    n = w.shape[2]
    tm = min(tm, t)
    tn = min(tn, n)
    assert t % tm == 0 and n % tn == 0
    vec = pl.BlockSpec((1, GATE_W), lambda i, j: (0, 0))
    return pl.pallas_call(
        _in_proj_kernel,
        grid=(t // tm, n // tn),
        in_specs=[pl.BlockSpec((tm, d), lambda i, j: (i, 0)),
                  pl.BlockSpec((None, d, tn), lambda i, j: (layer, 0, j)),
                  pl.BlockSpec((d, GATE_W), lambda i, j: (0, 0)), vec, vec, vec],
        out_specs=[pl.BlockSpec((tm, tn), lambda i, j: (i, j)),
                   pl.BlockSpec((tm, GATE_W), lambda i, j: (i, 0))],
        out_shape=[jax.ShapeDtypeStruct((t, n), BF16), jax.ShapeDtypeStruct((t, GATE_W), F32)],
        compiler_params=_cparams(("parallel", "arbitrary")),
        name="gdn_in_proj",
    )(x2, w, w_gate, is_a, neg_a, dtb)


def _each(fn, *lists):
    return [fn(*args) for args in zip(*lists)]


def _unit_lower_inverse_minus_identity(lows, same16, same32):
    c = lows[0].shape[0]
    d = _each(lambda low: jnp.where(same16, low, 0.0), lows)
    x = _each(lambda a: -a, d)
    p = _each(lambda a: _bdot(a, a), d)
    xp = _each(lambda a, b: _bdot(jnp.concatenate([a, b], axis=0), b), x, p)
    n = _each(lambda a, b, ab: a + b + ab[:c], x, p, xp)
    p2 = _each(lambda ab: ab[c:], xp)
    np2 = _each(lambda a, b: _bdot(jnp.concatenate([a, b], axis=0), b), n, p2)
    n = _each(lambda a, b, ab: a + b + ab[:c], n, p2, np2)
    p4 = _each(lambda ab: ab[c:], np2)
    n = _each(lambda a, b: a + b + _bdot(a, b), n, p4)
    only32 = jnp.logical_and(same32, jnp.logical_not(same16))
    off1 = _each(lambda low: jnp.where(only32, low, 0.0), lows)
    c1 = _each(lambda a, o: o + _bdot(a, o), n, off1)
    n = _each(lambda a, cc: a - (cc + _bdot(cc, a)), n, c1)
    off2 = _each(lambda low: jnp.where(same32, 0.0, low), lows)
    c2 = _each(lambda a, o: o + _bdot(a, o), n, off2)
    n = _each(lambda a, cc: a - (cc + _bdot(cc, a)), n, c2)
    return n


def _gdn_kernel(pq_ref, pk_ref, pv_ref, pz_ref, cwq_ref, cwk_ref, cwv_ref, gcol_ref, grow_ref,
                nw_ref, o_ref, hq_ref, hk_ref, hv_ref, state_ref, *, hb):
    c_idx = pl.program_id(2)
    C = CHUNK
    w_blk = hb * HEAD_W

    @pl.when(c_idx == 0)
    def _():
        state_ref[...] = jnp.zeros_like(state_ref)
        zero = jnp.zeros((C, w_blk), BF16)
        hq_ref[0:C, :] = zero
        hk_ref[0:C, :] = zero
        hv_ref[0:C, :] = zero

    sr = lax.broadcasted_iota(jnp.int32, ((CONV_TAPS - 1) * C, 2 * C), 0)
    sc = lax.broadcasted_iota(jnp.int32, ((CONV_TAPS - 1) * C, 2 * C), 1)
    tap = sr // C
    shift_mat = jnp.where(sc == sr - tap * (C - 1) + (C - (CONV_TAPS - 1)), 1.0, 0.0).astype(BF16)

    def conv_silu(p_ref, h_ref, cw_ref, rows):
        cur = p_ref[rows, :]
        h_ref[C:2 * C, :] = cur
        shifted = jnp.dot(shift_mat, h_ref[...], preferred_element_type=F32)
        w = cw_ref[...]
        y = shifted[0:C] * w[0:1, :]
        for j in range(1, CONV_TAPS - 1):
            y = y + shifted[j * C:(j + 1) * C] * w[j:j + 1, :]
        y = y + cur.astype(F32) * w[CONV_TAPS - 1:CONV_TAPS, :]
        h_ref[0:C, :] = cur
        return y * jax.nn.sigmoid(y)

    row = lax.broadcasted_iota(jnp.int32, (C, C), 0)
    col = lax.broadcasted_iota(jnp.int32, (C, C), 1)
    causal = row >= col
    strict = row > col
    same16 = (row // 16) == (col // 16)
    same32 = (row // 32) == (col // 32)
    tril = jnp.where(causal, 1.0, 0.0).astype(BF16)
    triu = jnp.where(row <= col, 1.0, 0.0).astype(BF16)

    nw = nw_ref[...]
    q_scale = HEAD_W ** -0.5

    def l2n(y, scale):
        return y * (lax.rsqrt(jnp.sum(y * y, axis=-1, keepdims=True) + GDN_EPS) * scale)

    def chunk(cc, carry):
        rows = pl.ds(pl.multiple_of(cc * C, C), C)
        yq = conv_silu(pq_ref, hq_ref, cwq_ref, rows)
        yk = conv_silu(pk_ref, hk_ref, cwk_ref, rows)
        yv = conv_silu(pv_ref, hv_ref, cwv_ref, rows)

        gcol = gcol_ref[rows, :]
        grow = grow_ref[cc]
        gh, gm, gl = _split3(gcol[:, hb:])
        gc_col = (jnp.dot(tril, gh, preferred_element_type=F32)
                  + jnp.dot(tril, gm, preferred_element_type=F32)
                  + jnp.dot(tril, gl, preferred_element_type=F32))
        rh, rm, rl = _split3(grow[hb:, :])
        gc_row = (jnp.dot(rh, triu, preferred_element_type=F32)
                  + jnp.dot(rm, triu, preferred_element_type=F32)
                  + jnp.dot(rl, triu, preferred_element_type=F32))

        heads = list(range(hb))
        sl = [slice(h * HEAD_W, (h + 1) * HEAD_W) for h in heads]
        q = [l2n(yq[:, s], q_scale) for s in sl]
        k = [l2n(yk[:, s], 1.0) for s in sl]
        v = [yv[:, s] for s in sl]
        beta = [gcol[:, h:h + 1] for h in heads]
        g_c = [gc_col[:, h:h + 1] for h in heads]
        g_r = [gc_row[h:h + 1, :] for h in heads]
        g_last = [r[:, C - 1:C] for r in g_r]
        decay = _each(lambda gc, gr: jnp.exp(jnp.where(causal, gc - gr, -jnp.inf)), g_c, g_r)
        eg = _each(jnp.exp, g_c)
        k_beta = _each(lambda a, b: a * b, k, beta)
        kq = _each(lambda kb, qq, kk: _bdot_nt(jnp.concatenate([kb, qq], axis=0), kk), k_beta, q, k)
        low = _each(lambda a, dc: jnp.where(strict, a[:C] * dc, 0.0), kq, decay)
        attn = _each(lambda a, dc: a[C:] * dc, kq, decay)
        n_inv = _unit_lower_inverse_minus_identity(low, same16, same32)
        rhs = _each(lambda vv, b, kb, e: jnp.concatenate([vv * b, kb * e], axis=1), v, beta, k_beta, eg)
        sol = _each(lambda r, n: r + _bdot(n, r), rhs, n_inv)
        state = [state_ref[h] for h in heads]
        ws = _each(lambda s, qq, e, st: _bdot(jnp.concatenate([s[:, HEAD_W:], qq * e], axis=0), st),
                   sol, q, eg, state)
        v_new = _each(lambda s, a: s[:, :HEAD_W] - a[:C], sol, ws)
        o = _each(lambda a, at, vn: a[C:] + _bdot(at, vn), ws, attn, v_new)
        k_dec = _each(lambda kk, gl_, gc: kk * jnp.exp(gl_ - gc), k, g_last, g_c)
        new_state = _each(lambda st, gl_, kd, vn: st * jnp.exp(gl_) + _bdot_tn(kd, vn),
                          state, g_last, k_dec, v_new)
        for h, st in zip(heads, new_state):
            state_ref[h] = st
        o = _each(lambda a: a * lax.rsqrt(jnp.mean(a * a, axis=-1, keepdims=True) + GDN_EPS) * nw, o)
        for s, a in zip(sl, o):
            zh = pz_ref[rows, s].astype(F32)
            o_ref[rows, s] = (a * (zh * jax.nn.sigmoid(zh))).astype(o_ref.dtype)
        return carry

    lax.fori_loop(0, pq_ref.shape[0] // C, chunk, 0)


def _gdn_core(p3, conv_w, gcol, grow, norm_w, hb, chunks_per_step):
    b, s, w4 = p3.shape
    assert p3.dtype == BF16
    nck = min(chunks_per_step, s // CHUNK)
    tc = nck * CHUNK
    assert s % tc == 0
    v_w = w4 // 4
    heads = v_w // HEAD_W
    groups = heads // hb
    wb = hb * HEAD_W

    def pspec(part):
        return pl.BlockSpec((None, tc, wb), lambda bi, gi, ci, part=part: (bi, ci, part * groups + gi))

    def cspec(part):
        return pl.BlockSpec((CONV_TAPS, wb), lambda bi, gi, ci, part=part: (0, part * groups + gi))

    halo = pltpu.VMEM((2 * CHUNK, wb), BF16)
    return pl.pallas_call(
        functools.partial(_gdn_kernel, hb=hb),
        grid=(b, groups, s // tc),
        in_specs=[pspec(0), pspec(1), pspec(2), pspec(3), cspec(0), cspec(1), cspec(2),
                  pl.BlockSpec((None, None, tc, 2 * hb), lambda bi, gi, ci: (bi, gi, ci, 0)),
                  pl.BlockSpec((None, None, nck, 2 * hb, CHUNK), lambda bi, gi, ci: (bi, gi, ci, 0, 0)),
                  pl.BlockSpec((1, HEAD_W), lambda bi, gi, ci: (0, 0))],
        out_specs=pl.BlockSpec((None, tc, wb), lambda bi, gi, ci: (bi, ci, gi)),
        out_shape=jax.ShapeDtypeStruct((b, s, v_w), BF16),
        scratch_shapes=[halo, halo, halo, pltpu.VMEM((hb, HEAD_W, HEAD_W), F32)],
        compiler_params=_cparams(("parallel", "parallel", "arbitrary")),
        name="gdn_core",
    )(p3, p3, p3, p3, conv_w, conv_w, conv_w, gcol, grow, norm_w)


def _layer_norm_rows(y, g, b):
    mu = jnp.mean(y, axis=-1, keepdims=True)
    yc = y - mu
    var = jnp.mean(yc * yc, axis=-1, keepdims=True)
    return yc * lax.rsqrt(var + LN_EPS) * g + b


def _row_chunks(rows):
    n = LN_ROW_CHUNKS if rows % (LN_ROW_CHUNKS * V7X_SUBLANES) == 0 else 1
    step = rows // n
    return [slice(k * step, (k + 1) * step) for k in range(n)]


def _matmul_then_ln(chunks, matmul_rows, ln_store_rows):
    pending = None
    for rs in chunks:
        h = matmul_rows(rs)
        if pending is not None:
            ln_store_rows(*pending)
        pending = (rs, h)
    ln_store_rows(*pending)


def _proj_ln_kernel(a_ref, w_ref, x_ref, g_ref, b_ref, o_ref, *, alpha):
    def matmul_rows(rs):
        return jnp.dot(a_ref[rs, :].astype(BF16), w_ref[...], preferred_element_type=F32)

    def ln_store_rows(rs, h):
        o_ref[rs, :] = _layer_norm_rows(alpha * x_ref[rs, :] + h, g_ref[...], b_ref[...])

    _matmul_then_ln(_row_chunks(a_ref.shape[0]), matmul_rows, ln_store_rows)


def _proj_ln(a, w, layer, x2, g, b, alpha, tm):
    t, k = a.shape
    d = w.shape[2]
    tm = min(tm, t)
    vec = pl.BlockSpec((1, d), lambda i: (0, 0))
    return pl.pallas_call(
        functools.partial(_proj_ln_kernel, alpha=alpha),
        grid=(t // tm,),
        in_specs=[pl.BlockSpec((tm, k), lambda i: (i, 0)),
                  pl.BlockSpec((None, k, d), lambda i: (layer, 0, 0)),
                  pl.BlockSpec((tm, d), lambda i: (i, 0)), vec, vec],
        out_specs=pl.BlockSpec((tm, d), lambda i: (i, 0)),
        out_shape=jax.ShapeDtypeStruct((t, d), F32),
        compiler_params=_cparams(("parallel",)),
        name="proj_ln",
    )(a, w, x2, g, b)


def _mlp_ln_kernel(x_ref, wu_ref, wd_ref, g_ref, b_ref, o_ref, xb_ref, acc_ref, *, alpha):
    f = pl.program_id(1)

    @pl.when(f == 0)
    def _():
        xb_ref[...] = x_ref[...].astype(BF16)
        acc_ref[...] = jnp.zeros_like(acc_ref)

    h = jnp.dot(xb_ref[...], wu_ref[...], preferred_element_type=F32)
    h = jnp.maximum(h, 0.0)
    h = h * h
    acc_ref[...] += jnp.dot(h.astype(BF16), wd_ref[...], preferred_element_type=F32)

    @pl.when(f == pl.num_programs(1) - 1)
    def _():
        o_ref[...] = _layer_norm_rows(alpha * x_ref[...] + acc_ref[...], g_ref[...], b_ref[...])


def _mlp_ln(x2, w_up, w_down, layer, g, b, alpha, tm, tf):
    t, d = x2.shape
    ff = w_up.shape[2]
    tm = min(tm, t)
    tf = min(tf, ff)
    vec = pl.BlockSpec((1, d), lambda i, f: (0, 0))
    return pl.pallas_call(
        functools.partial(_mlp_ln_kernel, alpha=alpha),
        grid=(t // tm, ff // tf),
        in_specs=[pl.BlockSpec((tm, d), lambda i, f: (i, 0)),
                  pl.BlockSpec((None, d, tf), lambda i, f: (layer, 0, f)),
                  pl.BlockSpec((None, tf, d), lambda i, f: (layer, f, 0)), vec, vec],
        out_specs=pl.BlockSpec((tm, d), lambda i, f: (i, 0)),
        out_shape=jax.ShapeDtypeStruct((t, d), F32),
        scratch_shapes=[pltpu.VMEM((tm, d), BF16), pltpu.VMEM((tm, d), F32)],
        compiler_params=_cparams(("parallel", "arbitrary")),
        name="mlp_ln",
    )(x2, w_up, w_down, g, b)


def _diff_attn_kernel(q_ref, k_ref, v_ref, lam_ref, sw_ref, o_ref, qs_ref, s_ref, acc_ref,
                      *, tq, tk, cw, nh, lambda_init):
    nq = q_ref.shape[0] // tq
    heads = list(range(nh))
    hs = [slice(h * HEAD_W, (h + 1) * HEAD_W) for h in heads]
    half = HEAD_W // 2
    lane = lax.broadcasted_iota(jnp.int32, (tq, HEAD_W), 1)
    ones_rows = jnp.ones((ONES_ROWS, tk), BF16)

    def stack_queries(i):
        qrows = pl.ds(pl.multiple_of(i * tq, tq), tq)
        for h in heads:
            q = q_ref[qrows, hs[h]]
            zero = jnp.zeros_like(q)
            qs_ref[h, 0:tq, :] = jnp.where(lane < half, q, zero)
            qs_ref[h, tq:2 * tq, :] = jnp.where(lane >= half, q, zero)

    def produce(slot, j, diag=None):
        rows = pl.ds(pl.multiple_of(j * tk, tk), tk)
        first = 0 if diag is None else diag * tk // cw * cw
        for h in heads:
            kb = k_ref[rows, hs[h]]
            for lo, hi in ([(0, 2 * tq)] if first == 0 else [(first, tq), (tq + first, 2 * tq)]):
                s = lax.dot_general(kb, qs_ref[h, lo:hi, :], (((1,), (1,)), ((), ())),
                                    preferred_element_type=F32)
                for c in range(lo // cw, hi // cw):
                    s_ref[h, slot, c] = s[:, c * cw - lo:(c + 1) * cw - lo]

    def consume(slot, j, m_prev, diag):
        rows = pl.ds(pl.multiple_of(j * tk, tk), tk)
        lhs = [jnp.concatenate([v_ref[rows, hs[h]].T, ones_rows], axis=0) for h in heads]
        m_parts = [[] for _ in heads]
        for c in range(2 * tq // cw):
            cs = slice(c * cw, (c + 1) * cw)
            r_min = (c * cw) % tq
            if diag is not None and diag * tk > r_min + cw - 1:
                for h in heads:
                    m_parts[h].append(m_prev[h][:, cs])
                continue
            nk = tk if diag is None else min(tk, r_min + cw - diag * tk)
            for h in heads:
                m_prev_c = m_prev[h][:, cs]
                sc = s_ref[h, slot, c, 0:nk, :]
                if diag is not None and diag * tk + nk - 1 > r_min:
                    key = lax.broadcasted_iota(jnp.int32, (nk, cw), 0) + diag * tk
                    qry = lax.broadcasted_iota(jnp.int32, (nk, cw), 1) + r_min
                    sc = jnp.where(key <= qry, sc, -jnp.inf)
                m_new_c = jnp.maximum(m_prev_c, jnp.max(sc, axis=0, keepdims=True))
                alpha_c = jnp.exp2(m_prev_c - m_new_c)
                p_c = jnp.exp2(sc - m_new_c).astype(BF16)
                pv_c = jnp.dot(lhs[h][:, 0:nk], p_c, preferred_element_type=F32)
                acc_ref[h, c] = alpha_c * acc_ref[h, c] + pv_c
                m_parts[h].append(m_new_c)
        return tuple(jnp.concatenate(parts, axis=1) for parts in m_parts)

    assert tq == 2 * tk

    def pair(jj, m):
        produce(1, 2 * jj + 1)
        m = consume(0, 2 * jj, m, None)
        produce(0, 2 * jj + 2)
        return consume(1, 2 * jj + 1, m, None)

    lp = lam_ref[...]
    lam = (jnp.exp(jnp.sum(lp[0:1, :] * lp[1:2, :], axis=-1, keepdims=True))
           - jnp.exp(jnp.sum(lp[2:3, :] * lp[3:4, :], axis=-1, keepdims=True)) + lambda_init)

    def finalize(i):
        per_comp = tq // cw

        def normalised(h, c):
            return acc_ref[h, c, 0:HEAD_W, :] * (1.0 / acc_ref[h, c, HEAD_W:HEAD_W + 1, :])

        for c in range(per_comp):
            qrows = pl.ds(pl.multiple_of(i * tq + c * cw, cw), cw)
            o = [(normalised(h, c) - lam * normalised(h, per_comp + c)).T for h in heads]
            o = [a * lax.rsqrt(jnp.mean(a * a, axis=-1, keepdims=True) + SUBLN_EPS) * sw_ref[...] for a in o]
            for h in heads:
                o_ref[qrows, hs[h]] = (o[h] * (1.0 - lambda_init)).astype(o_ref.dtype)

    def attend(i):
        m = lax.fori_loop(0, i, pair, tuple(jnp.full((1, 2 * tq), -jnp.inf, F32) for _ in heads))
        produce(1, 2 * i + 1, 1)
        m = consume(0, 2 * i, m, 0)
        consume(1, 2 * i + 1, m, 1)

    acc_ref[...] = jnp.ones_like(acc_ref)

    def query_block(i, carry):
        stack_queries(i)
        produce(0, 0)
        finalize(jnp.maximum(i - 1, 0))
        acc_ref[...] = jnp.zeros_like(acc_ref)
        attend(i)
        return carry

    lax.fori_loop(0, nq, query_block, 0)
    finalize(nq - 1)


def _diff_attention(q3, q_col0, kv3, heads, lam_params, subln_w, lambda_init, tq, tk, nh):
    b, s, _ = q3.shape
    wq = heads * HEAD_W
    tq = min(tq, s)
    tk = min(tk, tq)
    nh = min(nh, heads)
    groups = heads // nh
    wb = nh * HEAD_W
    cw = min(V7X_MXU_COLS, tq)
    assert s % tq == 0 and tq == 2 * tk and tq % cw == 0 and q_col0 % wb == 0
    q_blk0 = q_col0 // wb
    return pl.pallas_call(
        functools.partial(_diff_attn_kernel, tq=tq, tk=tk, cw=cw, nh=nh, lambda_init=lambda_init),
        grid=(b, groups),
        in_specs=[pl.BlockSpec((None, s, wb), lambda bi, g: (bi, 0, q_blk0 + g)),
                  pl.BlockSpec((None, s, wb), lambda bi, g: (bi, 0, g)),
                  pl.BlockSpec((None, s, wb), lambda bi, g, groups=groups: (bi, 0, groups + g)),
                  pl.BlockSpec(lam_params.shape, lambda bi, g: (0, 0)),
                  pl.BlockSpec((1, HEAD_W), lambda bi, g: (0, 0))],
        out_specs=pl.BlockSpec((None, s, wb), lambda bi, g: (bi, 0, g)),
        out_shape=jax.ShapeDtypeStruct((b, s, wq), BF16),
        scratch_shapes=[pltpu.VMEM((nh, 2 * tq, HEAD_W), BF16),
                        pltpu.VMEM((nh, 2, 2 * tq // cw, tk, cw), F32),
                        pltpu.VMEM((nh, 2 * tq // cw, HEAD_W + ONES_ROWS, cw), F32)],
        compiler_params=_cparams(("parallel", "parallel")),
        name="diff_attn",
    )(q3, kv3, kv3, lam_params, subln_w)


def _gdn_layer(x2, bsz, seq, layer, w_in_f32, w_in_b16, conv_w, a_log, dt_bias, norm_w, w_out_b16, ln_g, ln_b,
               alpha, hb, tiles):
    t, d = x2.shape
    heads = a_log.shape[0]
    v_w = heads * HEAD_W
    groups = heads // hb
    main = 4 * v_w
    wb = w_in_f32[layer, :, main:main + heads].reshape(d, groups, hb)
    wa = w_in_f32[layer, :, main + heads:main + 2 * heads].reshape(d, groups, hb)
    w_gate = jnp.concatenate([wb, wa], axis=2).reshape(d, 2 * heads)
    w_gate = jnp.pad(w_gate, ((0, 0), (0, GATE_W - 2 * heads))).astype(BF16)

    def per_col(vals, fill):
        zeros = jnp.full((groups, hb), fill, F32)
        cols = jnp.concatenate([zeros, vals.astype(F32).reshape(groups, hb)], axis=1).reshape(1, 2 * heads)
        return jnp.pad(cols, ((0, 0), (0, GATE_W - 2 * heads)), constant_values=fill)

    is_a = per_col(jnp.ones((heads,), F32), 0.0)
    neg_a = per_col(-jnp.exp(a_log.astype(F32)), 0.0)
    dtb = per_col(dt_bias, 0.0)
    p, gates = _in_proj(x2, w_in_b16, layer, w_gate, is_a, neg_a, dtb, tiles["tm"], tiles["tn"])
    gates = gates[:, :2 * heads]
    gcol = gates.reshape(bsz, seq, groups, 2 * hb).transpose(0, 2, 1, 3)
    grow = gates.reshape(bsz, seq // CHUNK, CHUNK, groups, 2 * hb).transpose(0, 3, 1, 4, 2)
    o = _gdn_core(p.reshape(bsz, seq, main), conv_w, gcol, grow, norm_w.reshape(1, HEAD_W), hb,
                  tiles["gdn_chunks"])
    return _proj_ln(o.reshape(t, v_w), w_out_b16, layer, x2, ln_g.reshape(1, d), ln_b.reshape(1, d),
                    alpha, tiles["tm_proj"])


def _tiles():
    return dict(tm=1024, tn=2048, tm_proj=512, tm_mlp=512, tf_mlp=1024, tq=1024, tk=512, nh=2, hb=16,
                gdn_chunks=8)


def _forward(x, gdn_w_in, gdn_conv_w, gdn_a_log, gdn_dt_bias, gdn_norm_w, gdn_w_out, diff_w_q, diff_lambda,
             diff_subln_w, diff_w_o, shared_w_kv, mlp_w_up, mlp_w_down, ln_g, ln_b, tiles):
    bsz, seq, d = x.shape
    t = bsz * seq
    depth = mlp_w_up.shape[0]
    n_a = gdn_w_in.shape[0]
    alpha = (2 * depth) ** 0.25
    x2 = x.reshape(t, d).astype(F32)
    w_in_b16 = gdn_w_in[:, :, :4 * gdn_a_log.shape[1] * HEAD_W].astype(BF16)
    w_out_b16 = gdn_w_out.astype(BF16)
    w_q_b16 = diff_w_q.astype(BF16)
    w_o_b16 = diff_w_o.astype(BF16)
    w_up_b16 = mlp_w_up.astype(BF16)
    w_down_b16 = mlp_w_down.astype(BF16)
    dk = diff_lambda.shape[-1]
    q_scale = dk ** -0.5 * math.log2(math.e)
    kv_w = shared_w_kv.shape[1]
    attn_heads = diff_w_q.shape[2] // HEAD_W
    w_kvq_b16 = jnp.concatenate([shared_w_kv.astype(BF16), w_q_b16[0]], axis=1)
    kvq_scale = jnp.concatenate([jnp.ones((1, kv_w), F32), jnp.full((1, diff_w_q.shape[2]), q_scale, F32)], axis=1)
    kv3 = None
    for l in range(depth):
        if l < n_a:
            x2 = _gdn_layer(x2, bsz, seq, l, gdn_w_in, w_in_b16, gdn_conv_w[l], gdn_a_log[l], gdn_dt_bias[l],
                            gdn_norm_w[l], w_out_b16, ln_g[l, 0], ln_b[l, 0], alpha,
                            min(tiles["hb"], gdn_a_log.shape[1]), tiles)
        else:
            j = l - n_a
            lambda_init = 0.8 - 0.6 * math.exp(-0.3 * l)
            if j == 0:
                q3, q_col0 = kv3, kv_w
            else:
                q = _matmul(x2, w_q_b16, j, BF16, tiles["tm"], tiles["tn"], scale=q_scale, name="diff_q_proj")
                q3, q_col0 = q.reshape(bsz, seq, -1), 0
            o = _diff_attention(q3, q_col0, kv3, attn_heads, diff_lambda[j].astype(F32),
                                diff_subln_w[j].reshape(1, HEAD_W).astype(F32), lambda_init, tiles["tq"],
                                tiles["tk"], tiles["nh"])
            x2 = _proj_ln(o.reshape(t, -1), w_o_b16, j, x2, ln_g[l, 0].reshape(1, d),
                          ln_b[l, 0].reshape(1, d), alpha, tiles["tm_proj"])
        x2 = _mlp_ln(x2, w_up_b16, w_down_b16, l, ln_g[l, 1].reshape(1, d),
                     ln_b[l, 1].reshape(1, d), alpha, tiles["tm_mlp"], tiles["tf_mlp"])
        if l == n_a - 1:
            kvq = _matmul_colscale(x2, w_kvq_b16, kvq_scale, BF16, tiles["tm"], tiles["tn"], "shared_kvq_proj")
            kv3 = kvq.reshape(bsz, seq, -1)
    return x2.reshape(bsz, seq, d).astype(x.dtype)


def kernel(x, gdn_w_in, gdn_conv_w, gdn_a_log, gdn_dt_bias, gdn_norm_w, gdn_w_out, diff_w_q, diff_lambda,
           diff_subln_w, diff_w_o, shared_w_kv, mlp_w_up, mlp_w_down, ln_g, ln_b):
    return _forward(x, gdn_w_in, gdn_conv_w, gdn_a_log, gdn_dt_bias, gdn_norm_w, gdn_w_out, diff_w_q,
                    diff_lambda, diff_subln_w, diff_w_o, shared_w_kv, mlp_w_up, mlp_w_down, ln_g, ln_b,
                    _tiles())
```

```python
import functools
import math

import jax
import jax.numpy as jnp
from jax import lax
from jax.experimental import pallas as pl
from jax.experimental.pallas import tpu as pltpu

F32 = jnp.float32
BF16 = jnp.bfloat16

HEAD_W = 128
CHUNK = 64
CONV_TAPS = 4
GDN_EPS = 1e-6
SUBLN_EPS = 1e-5
LN_EPS = 1e-5
GATE_W = 128
LN_ROW_CHUNKS = 4

V7X_SUBLANES = 8
V7X_MXU_COLS = 256
ONES_ROWS = 16
V7X_VMEM_LIMIT_BYTES = 56 * 1024 * 1024


def _cparams(sem):
    return pltpu.CompilerParams(dimension_semantics=sem, vmem_limit_bytes=V7X_VMEM_LIMIT_BYTES)


def _bdot(a, b):
    return jnp.dot(a.astype(BF16), b.astype(BF16), preferred_element_type=F32)


def _bdot_nt(a, b):
    return lax.dot_general(a.astype(BF16), b.astype(BF16), (((1,), (1,)), ((), ())),
                           preferred_element_type=F32)


def _bdot_tn(a, b):
    return lax.dot_general(a.astype(BF16), b.astype(BF16), (((0,), (0,)), ((), ())),
                           preferred_element_type=F32)


def _split3(x):
    h = x.astype(BF16)
    r = x - h.astype(F32)
    m = r.astype(BF16)
    l = (r - m.astype(F32)).astype(BF16)
    return h, m, l


def _mm_kernel(x_ref, w_ref, o_ref, *, scale):
    acc = jnp.dot(x_ref[...].astype(BF16), w_ref[...], preferred_element_type=F32)
    if scale != 1.0:
        acc = acc * scale
    o_ref[...] = acc.astype(o_ref.dtype)


def _mm_colscale_kernel(x_ref, w_ref, s_ref, o_ref):
    acc = jnp.dot(x_ref[...].astype(BF16), w_ref[...], preferred_element_type=F32)
    o_ref[...] = (acc * s_ref[...]).astype(o_ref.dtype)


def _matmul_colscale(x, w, col_scale, out_dtype, tm, tn, name):
    m, k = x.shape
    n = w.shape[1]
    tm = min(tm, m)
    tn = min(tn, n)
    assert m % tm == 0 and n % tn == 0
    return pl.pallas_call(
        _mm_colscale_kernel,
        grid=(m // tm, n // tn),
        in_specs=[pl.BlockSpec((tm, k), lambda i, j: (i, 0)),
                  pl.BlockSpec((k, tn), lambda i, j: (0, j)),
                  pl.BlockSpec((1, tn), lambda i, j: (0, j))],
        out_specs=pl.BlockSpec((tm, tn), lambda i, j: (i, j)),
        out_shape=jax.ShapeDtypeStruct((m, n), out_dtype),
        compiler_params=_cparams(("parallel", "parallel")),
        name=name,
    )(x, w, col_scale)


def _matmul(x, w, layer, out_dtype, tm, tn, scale=1.0, name="matmul"):
    m, k = x.shape
    n = w.shape[2]
    tm = min(tm, m)
    tn = min(tn, n)
    assert m % tm == 0 and n % tn == 0
    return pl.pallas_call(
        functools.partial(_mm_kernel, scale=scale),
        grid=(m // tm, n // tn),
        in_specs=[pl.BlockSpec((tm, k), lambda i, j: (i, 0)),
                  pl.BlockSpec((None, k, tn), lambda i, j: (layer, 0, j))],
        out_specs=pl.BlockSpec((tm, tn), lambda i, j: (i, j)),
        out_shape=jax.ShapeDtypeStruct((m, n), out_dtype),
        compiler_params=_cparams(("parallel", "parallel")),
        name=name,
    )(x, w)


def _in_proj_kernel(x_ref, w_ref, wg_ref, isa_ref, nega_ref, dtb_ref, p_ref, g_ref):
    p_ref[...] = jnp.dot(x_ref[...].astype(BF16), w_ref[...], preferred_element_type=F32).astype(p_ref.dtype)

    @pl.when(pl.program_id(1) == 0)
    def _():
        acc = jnp.dot(x_ref[...].astype(BF16), wg_ref[...], preferred_element_type=F32)
        beta = jax.nn.sigmoid(acc)
        s = acc + dtb_ref[...]
        softplus = jnp.maximum(s, 0.0) + jnp.log1p(jnp.exp(-jnp.abs(s)))
        g = nega_ref[...] * softplus
        g_ref[...] = jnp.where(isa_ref[...] > 0.5, g, beta)


def _in_proj(x2, w, layer, n, w_gate, is_a, neg_a, dtb, tm, tn):
    t, d = x2.shape
    tm = min(tm, t)
    tn = min(tn, n)
    assert t % tm == 0 and n % tn == 0
    vec = pl.BlockSpec((1, GATE_W), lambda i, j: (0, 0))
    return pl.pallas_call(
        _in_proj_kernel,
        grid=(t // tm, n // tn),
        in_specs=[pl.BlockSpec((tm, d), lambda i, j: (i, 0)),
                  pl.BlockSpec((None, d, tn), lambda i, j: (layer, 0, j)),
                  pl.BlockSpec((d, GATE_W), lambda i, j: (0, 0)), vec, vec, vec],
        out_specs=[pl.BlockSpec((tm, tn), lambda i, j: (i, j)),
                   pl.BlockSpec((tm, GATE_W), lambda i, j: (i, 0))],
        out_shape=[jax.ShapeDtypeStruct((t, n), BF16), jax.ShapeDtypeStruct((t, GATE_W), F32)],
        compiler_params=_cparams(("parallel", "arbitrary")),
        name="gdn_in_proj",
    )(x2, w, w_gate, is_a, neg_a, dtb)


def _each(fn, *lists):
    return [fn(*args) for args in zip(*lists)]


def _unit_lower_inverse_minus_identity(lows, same16, same32):
    c = lows[0].shape[0]
    d = _each(lambda low: jnp.where(same16, low, 0.0), lows)
    x = _each(lambda a: -a, d)
    p = _each(lambda a: _bdot(a, a), d)
    xp = _each(lambda a, b: _bdot(jnp.concatenate([a, b], axis=0), b), x, p)
    n = _each(lambda a, b, ab: a + b + ab[:c], x, p, xp)
    p2 = _each(lambda ab: ab[c:], xp)
    np2 = _each(lambda a, b: _bdot(jnp.concatenate([a, b], axis=0), b), n, p2)
    n = _each(lambda a, b, ab: a + b + ab[:c], n, p2, np2)
    p4 = _each(lambda ab: ab[c:], np2)
    n = _each(lambda a, b: a + b + _bdot(a, b), n, p4)
    only32 = jnp.logical_and(same32, jnp.logical_not(same16))
    off1 = _each(lambda low: jnp.where(only32, low, 0.0), lows)
    c1 = _each(lambda a, o: o + _bdot(a, o), n, off1)
    n = _each(lambda a, cc: a - (cc + _bdot(cc, a)), n, c1)
    off2 = _each(lambda low: jnp.where(same32, 0.0, low), lows)
    c2 = _each(lambda a, o: o + _bdot(a, o), n, off2)
    n = _each(lambda a, cc: a - (cc + _bdot(cc, a)), n, c2)
    return n


def _gdn_kernel(pq_ref, pk_ref, pv_ref, pz_ref, cwq_ref, cwk_ref, cwv_ref, gcol_ref, grow_ref,
                nw_ref, o_ref, hq_ref, hk_ref, hv_ref, state_ref, *, hb):
    c_idx = pl.program_id(2)
    C = CHUNK
    w_blk = hb * HEAD_W

    @pl.when(c_idx == 0)
    def _():
        state_ref[...] = jnp.zeros_like(state_ref)
        zero = jnp.zeros((C, w_blk), BF16)
        hq_ref[0:C, :] = zero
        hk_ref[0:C, :] = zero
        hv_ref[0:C, :] = zero

    sr = lax.broadcasted_iota(jnp.int32, ((CONV_TAPS - 1) * C, 2 * C), 0)
    sc = lax.broadcasted_iota(jnp.int32, ((CONV_TAPS - 1) * C, 2 * C), 1)
    tap = sr // C
    shift_mat = jnp.where(sc == sr - tap * (C - 1) + (C - (CONV_TAPS - 1)), 1.0, 0.0).astype(BF16)

    def conv_silu(p_ref, h_ref, cw_ref, rows):
        cur = p_ref[rows, :]
        h_ref[C:2 * C, :] = cur
        shifted = jnp.dot(shift_mat, h_ref[...], preferred_element_type=F32)
        w = cw_ref[...]
        y = shifted[0:C] * w[0:1, :]
        for j in range(1, CONV_TAPS - 1):
            y = y + shifted[j * C:(j + 1) * C] * w[j:j + 1, :]
        y = y + cur.astype(F32) * w[CONV_TAPS - 1:CONV_TAPS, :]
        h_ref[0:C, :] = cur
        return y * jax.nn.sigmoid(y)

    row = lax.broadcasted_iota(jnp.int32, (C, C), 0)
    col = lax.broadcasted_iota(jnp.int32, (C, C), 1)
    causal = row >= col
    strict = row > col
    same16 = (row // 16) == (col // 16)
    same32 = (row // 32) == (col // 32)
    tril = jnp.where(causal, 1.0, 0.0).astype(BF16)
    triu = jnp.where(row <= col, 1.0, 0.0).astype(BF16)

    nw = nw_ref[...]
    q_scale = HEAD_W ** -0.5

    def l2n(y, scale):
        return y * (lax.rsqrt(jnp.sum(y * y, axis=-1, keepdims=True) + GDN_EPS) * scale)

    def chunk(cc, carry):
        rows = pl.ds(pl.multiple_of(cc * C, C), C)
        yq = conv_silu(pq_ref, hq_ref, cwq_ref, rows)
        yk = conv_silu(pk_ref, hk_ref, cwk_ref, rows)
        yv = conv_silu(pv_ref, hv_ref, cwv_ref, rows)

        gcol = gcol_ref[rows, :]
        grow = grow_ref[cc]
        gh, gm, gl = _split3(gcol[:, hb:])
        gc_col = (jnp.dot(tril, gh, preferred_element_type=F32)
                  + jnp.dot(tril, gm, preferred_element_type=F32)
                  + jnp.dot(tril, gl, preferred_element_type=F32))
        rh, rm, rl = _split3(grow[hb:, :])
        gc_row = (jnp.dot(rh, triu, preferred_element_type=F32)
                  + jnp.dot(rm, triu, preferred_element_type=F32)
                  + jnp.dot(rl, triu, preferred_element_type=F32))

        heads = list(range(hb))
        sl = [slice(h * HEAD_W, (h + 1) * HEAD_W) for h in heads]
        q = [l2n(yq[:, s], q_scale) for s in sl]
        k = [l2n(yk[:, s], 1.0) for s in sl]
        v = [yv[:, s] for s in sl]
        beta = [gcol[:, h:h + 1] for h in heads]
        g_c = [gc_col[:, h:h + 1] for h in heads]
        g_r = [gc_row[h:h + 1, :] for h in heads]
        g_last = [r[:, C - 1:C] for r in g_r]
        decay = _each(lambda gc, gr: jnp.exp(jnp.where(causal, gc - gr, -jnp.inf)), g_c, g_r)
        eg = _each(jnp.exp, g_c)
        k_beta = _each(lambda a, b: a * b, k, beta)
        kq = _each(lambda kb, qq, kk: _bdot_nt(jnp.concatenate([kb, qq], axis=0), kk), k_beta, q, k)
        low = _each(lambda a, dc: jnp.where(strict, a[:C] * dc, 0.0), kq, decay)
        attn = _each(lambda a, dc: a[C:] * dc, kq, decay)
        n_inv = _unit_lower_inverse_minus_identity(low, same16, same32)
        rhs = _each(lambda vv, b, kb, e: jnp.concatenate([vv * b, kb * e], axis=1), v, beta, k_beta, eg)
        sol = _each(lambda r, n: r + _bdot(n, r), rhs, n_inv)
        state = [state_ref[h] for h in heads]
        ws = _each(lambda s, qq, e, st: _bdot(jnp.concatenate([s[:, HEAD_W:], qq * e], axis=0), st),
                   sol, q, eg, state)
        v_new = _each(lambda s, a: s[:, :HEAD_W] - a[:C], sol, ws)
        o = _each(lambda a, at, vn: a[C:] + _bdot(at, vn), ws, attn, v_new)
        k_dec = _each(lambda kk, gl_, gc: kk * jnp.exp(gl_ - gc), k, g_last, g_c)
        new_state = _each(lambda st, gl_, kd, vn: st * jnp.exp(gl_) + _bdot_tn(kd, vn),
                          state, g_last, k_dec, v_new)
        for h, st in zip(heads, new_state):
            state_ref[h] = st
        o = _each(lambda a: a * lax.rsqrt(jnp.mean(a * a, axis=-1, keepdims=True) + GDN_EPS) * nw, o)
        for s, a in zip(sl, o):
            zh = pz_ref[rows, s].astype(F32)
            o_ref[rows, s] = (a * (zh * jax.nn.sigmoid(zh))).astype(o_ref.dtype)
        return carry

    lax.fori_loop(0, pq_ref.shape[0] // C, chunk, 0)


def _gdn_core(p3, conv_w, gcol, grow, norm_w, hb, chunks_per_step):
    b, s, w4 = p3.shape
    assert p3.dtype == BF16
    nck = min(chunks_per_step, s // CHUNK)
    tc = nck * CHUNK
    assert s % tc == 0
    v_w = w4 // 4
    heads = v_w // HEAD_W
    groups = heads // hb
    wb = hb * HEAD_W

    def pspec(part):
        return pl.BlockSpec((None, tc, wb), lambda bi, gi, ci, part=part: (bi, ci, part * groups + gi))

    def cspec(part):
        return pl.BlockSpec((CONV_TAPS, wb), lambda bi, gi, ci, part=part: (0, part * groups + gi))

    halo = pltpu.VMEM((2 * CHUNK, wb), BF16)
    return pl.pallas_call(
        functools.partial(_gdn_kernel, hb=hb),
        grid=(b, groups, s // tc),
        in_specs=[pspec(0), pspec(1), pspec(2), pspec(3), cspec(0), cspec(1), cspec(2),
                  pl.BlockSpec((None, None, tc, 2 * hb), lambda bi, gi, ci: (bi, gi, ci, 0)),
                  pl.BlockSpec((None, None, nck, 2 * hb, CHUNK), lambda bi, gi, ci: (bi, gi, ci, 0, 0)),
                  pl.BlockSpec((1, HEAD_W), lambda bi, gi, ci: (0, 0))],
        out_specs=pl.BlockSpec((None, tc, wb), lambda bi, gi, ci: (bi, ci, gi)),
        out_shape=jax.ShapeDtypeStruct((b, s, v_w), BF16),
        scratch_shapes=[halo, halo, halo, pltpu.VMEM((hb, HEAD_W, HEAD_W), F32)],
        compiler_params=_cparams(("parallel", "parallel", "arbitrary")),
        name="gdn_core",
    )(p3, p3, p3, p3, conv_w, conv_w, conv_w, gcol, grow, norm_w)


def _layer_norm_rows(y, g, b):
    mu = jnp.mean(y, axis=-1, keepdims=True)
    yc = y - mu
    var = jnp.mean(yc * yc, axis=-1, keepdims=True)
    return yc * lax.rsqrt(var + LN_EPS) * g + b


def _row_chunks(rows):
    n = LN_ROW_CHUNKS if rows % (LN_ROW_CHUNKS * V7X_SUBLANES) == 0 else 1
    step = rows // n
    return [slice(k * step, (k + 1) * step) for k in range(n)]


def _matmul_then_ln(chunks, matmul_rows, ln_store_rows):
    pending = None
    for rs in chunks:
        h = matmul_rows(rs)
        if pending is not None:
            ln_store_rows(*pending)
        pending = (rs, h)
    ln_store_rows(*pending)


def _proj_ln_kernel(a_ref, w_ref, x_ref, g_ref, b_ref, o_ref, *, alpha):
    def matmul_rows(rs):
        return jnp.dot(a_ref[rs, :].astype(BF16), w_ref[...], preferred_element_type=F32)

    def ln_store_rows(rs, h):
        o_ref[rs, :] = _layer_norm_rows(alpha * x_ref[rs, :] + h, g_ref[...], b_ref[...])

    _matmul_then_ln(_row_chunks(a_ref.shape[0]), matmul_rows, ln_store_rows)


def _proj_ln(a, w, layer, x2, g, b, alpha, tm):
    t, k = a.shape
    d = w.shape[2]
    tm = min(tm, t)
    vec = pl.BlockSpec((1, d), lambda i: (0, 0))
    return pl.pallas_call(
        functools.partial(_proj_ln_kernel, alpha=alpha),
        grid=(t // tm,),
        in_specs=[pl.BlockSpec((tm, k), lambda i: (i, 0)),
                  pl.BlockSpec((None, k, d), lambda i: (layer, 0, 0)),
                  pl.BlockSpec((tm, d), lambda i: (i, 0)), vec, vec],
        out_specs=pl.BlockSpec((tm, d), lambda i: (i, 0)),
        out_shape=jax.ShapeDtypeStruct((t, d), F32),
        compiler_params=_cparams(("parallel",)),
        name="proj_ln",
    )(a, w, x2, g, b)


def _mlp_ln_kernel(x_ref, wu_ref, wd_ref, g_ref, b_ref, o_ref, xb_ref, acc_ref, *, alpha):
    f = pl.program_id(1)

    @pl.when(f == 0)
    def _():
        xb_ref[...] = x_ref[...].astype(BF16)
        acc_ref[...] = jnp.zeros_like(acc_ref)

    h = jnp.dot(xb_ref[...], wu_ref[...], preferred_element_type=F32)
    h = jnp.maximum(h, 0.0)
    h = h * h
    acc_ref[...] += jnp.dot(h.astype(BF16), wd_ref[...], preferred_element_type=F32)

    @pl.when(f == pl.num_programs(1) - 1)
    def _():
        o_ref[...] = _layer_norm_rows(alpha * x_ref[...] + acc_ref[...], g_ref[...], b_ref[...])


def _mlp_ln(x2, w_up, w_down, layer, g, b, alpha, tm, tf):
    t, d = x2.shape
    ff = w_up.shape[2]
    tm = min(tm, t)
    tf = min(tf, ff)
    vec = pl.BlockSpec((1, d), lambda i, f: (0, 0))
    return pl.pallas_call(
        functools.partial(_mlp_ln_kernel, alpha=alpha),
        grid=(t // tm, ff // tf),
        in_specs=[pl.BlockSpec((tm, d), lambda i, f: (i, 0)),
                  pl.BlockSpec((None, d, tf), lambda i, f: (layer, 0, f)),
                  pl.BlockSpec((None, tf, d), lambda i, f: (layer, f, 0)), vec, vec],
        out_specs=pl.BlockSpec((tm, d), lambda i, f: (i, 0)),
        out_shape=jax.ShapeDtypeStruct((t, d), F32),
        scratch_shapes=[pltpu.VMEM((tm, d), BF16), pltpu.VMEM((tm, d), F32)],
        compiler_params=_cparams(("parallel", "arbitrary")),
        name="mlp_ln",
    )(x2, w_up, w_down, g, b)


def _diff_attn_kernel(q_ref, k_ref, v_ref, lam_ref, sw_ref, o_ref, qs_ref, s_ref, acc_ref,
                      *, tq, tk, cw, nh, lambda_init):
    nq = q_ref.shape[0] // tq
    heads = list(range(nh))
    hs = [slice(h * HEAD_W, (h + 1) * HEAD_W) for h in heads]
    half = HEAD_W // 2
    lane = lax.broadcasted_iota(jnp.int32, (tq, HEAD_W), 1)
    ones_rows = jnp.ones((ONES_ROWS, tk), BF16)

    def stack_queries(i):
        qrows = pl.ds(pl.multiple_of(i * tq, tq), tq)
        for h in heads:
            q = q_ref[qrows, hs[h]]
            zero = jnp.zeros_like(q)
            qs_ref[h, 0:tq, :] = jnp.where(lane < half, q, zero)
            qs_ref[h, tq:2 * tq, :] = jnp.where(lane >= half, q, zero)

    def produce(slot, j, diag=None):
        rows = pl.ds(pl.multiple_of(j * tk, tk), tk)
        first = 0 if diag is None else diag * tk // cw * cw
        for h in heads:
            kb = k_ref[rows, hs[h]]
            for lo, hi in ([(0, 2 * tq)] if first == 0 else [(first, tq), (tq + first, 2 * tq)]):
                s = lax.dot_general(kb, qs_ref[h, lo:hi, :], (((1,), (1,)), ((), ())),
                                    preferred_element_type=F32)
                for c in range(lo // cw, hi // cw):
                    s_ref[h, slot, c] = s[:, c * cw - lo:(c + 1) * cw - lo]

    def consume(slot, j, m_prev, diag):
        rows = pl.ds(pl.multiple_of(j * tk, tk), tk)
        lhs = [jnp.concatenate([v_ref[rows, hs[h]].T, ones_rows], axis=0) for h in heads]
        m_parts = [[] for _ in heads]
        for c in range(2 * tq // cw):
            cs = slice(c * cw, (c + 1) * cw)
            r_min = (c * cw) % tq
            if diag is not None and diag * tk > r_min + cw - 1:
                for h in heads:
                    m_parts[h].append(m_prev[h][:, cs])
                continue
            nk = tk if diag is None else min(tk, r_min + cw - diag * tk)
            for h in heads:
                m_prev_c = m_prev[h][:, cs]
                sc = s_ref[h, slot, c, 0:nk, :]
                if diag is not None and diag * tk + nk - 1 > r_min:
                    key = lax.broadcasted_iota(jnp.int32, (nk, cw), 0) + diag * tk
                    qry = lax.broadcasted_iota(jnp.int32, (nk, cw), 1) + r_min
                    sc = jnp.where(key <= qry, sc, -jnp.inf)
                m_new_c = jnp.maximum(m_prev_c, jnp.max(sc, axis=0, keepdims=True))
                alpha_c = jnp.exp2(m_prev_c - m_new_c)
                p_c = jnp.exp2(sc - m_new_c).astype(BF16)
                pv_c = jnp.dot(lhs[h][:, 0:nk], p_c, preferred_element_type=F32)
                acc_ref[h, c] = alpha_c * acc_ref[h, c] + pv_c
                m_parts[h].append(m_new_c)
        return tuple(jnp.concatenate(parts, axis=1) for parts in m_parts)

    assert tq == 2 * tk

    def pair(jj, m):
        produce(1, 2 * jj + 1)
        m = consume(0, 2 * jj, m, None)
        produce(0, 2 * jj + 2)
        return consume(1, 2 * jj + 1, m, None)

    lp = lam_ref[...]
    lam = (jnp.exp(jnp.sum(lp[0:1, :] * lp[1:2, :], axis=-1, keepdims=True))
           - jnp.exp(jnp.sum(lp[2:3, :] * lp[3:4, :], axis=-1, keepdims=True)) + lambda_init)

    def finalize(i):
        per_comp = tq // cw

        def normalised(h, c):
            return acc_ref[h, c, 0:HEAD_W, :] * (1.0 / acc_ref[h, c, HEAD_W:HEAD_W + 1, :])

        for c in range(per_comp):
            qrows = pl.ds(pl.multiple_of(i * tq + c * cw, cw), cw)
            o = [(normalised(h, c) - lam * normalised(h, per_comp + c)).T for h in heads]
            o = [a * lax.rsqrt(jnp.mean(a * a, axis=-1, keepdims=True) + SUBLN_EPS) * sw_ref[...] for a in o]
            for h in heads:
                o_ref[qrows, hs[h]] = (o[h] * (1.0 - lambda_init)).astype(o_ref.dtype)

    def attend(i):
        m = lax.fori_loop(0, i, pair, tuple(jnp.full((1, 2 * tq), -jnp.inf, F32) for _ in heads))
        produce(1, 2 * i + 1, 1)
        m = consume(0, 2 * i, m, 0)
        consume(1, 2 * i + 1, m, 1)

    acc_ref[...] = jnp.ones_like(acc_ref)

    def query_block(i, carry):
        stack_queries(i)
        produce(0, 0)
        finalize(jnp.maximum(i - 1, 0))
        acc_ref[...] = jnp.zeros_like(acc_ref)
        attend(i)
        return carry

    lax.fori_loop(0, nq, query_block, 0)
    finalize(nq - 1)


def _diff_attention(q3, q_col0, kv3, heads, lam_params, subln_w, lambda_init, tq, tk, nh):
    b, s, _ = q3.shape
    wq = heads * HEAD_W
    tq = min(tq, s)
    tk = min(tk, tq)
    nh = min(nh, heads)
    groups = heads // nh
    wb = nh * HEAD_W
    cw = min(V7X_MXU_COLS, tq)
    assert s % tq == 0 and tq == 2 * tk and tq % cw == 0 and q_col0 % wb == 0
    q_blk0 = q_col0 // wb
    return pl.pallas_call(
        functools.partial(_diff_attn_kernel, tq=tq, tk=tk, cw=cw, nh=nh, lambda_init=lambda_init),
        grid=(b, groups),
        in_specs=[pl.BlockSpec((None, s, wb), lambda bi, g: (bi, 0, q_blk0 + g)),
                  pl.BlockSpec((None, s, wb), lambda bi, g: (bi, 0, g)),
                  pl.BlockSpec((None, s, wb), lambda bi, g, groups=groups: (bi, 0, groups + g)),
                  pl.BlockSpec(lam_params.shape, lambda bi, g: (0, 0)),
                  pl.BlockSpec((1, HEAD_W), lambda bi, g: (0, 0))],
        out_specs=pl.BlockSpec((None, s, wb), lambda bi, g: (bi, 0, g)),
        out_shape=jax.ShapeDtypeStruct((b, s, wq), BF16),
        scratch_shapes=[pltpu.VMEM((nh, 2 * tq, HEAD_W), BF16),
                        pltpu.VMEM((nh, 2, 2 * tq // cw, tk, cw), F32),
                        pltpu.VMEM((nh, 2 * tq // cw, HEAD_W + ONES_ROWS, cw), F32)],
        compiler_params=_cparams(("parallel", "parallel")),
        name="diff_attn",
    )(q3, kv3, kv3, lam_params, subln_w)


def _gdn_layer(x2, bsz, seq, layer, w_in_f32, w_in_b16, conv_w, a_log, dt_bias, norm_w, w_out_b16, ln_g, ln_b,
               alpha, hb, tiles):
    t, d = x2.shape
    heads = a_log.shape[0]
    v_w = heads * HEAD_W
    groups = heads // hb
    main = 4 * v_w
    wb = w_in_f32[layer, :, main:main + heads].reshape(d, groups, hb)
    wa = w_in_f32[layer, :, main + heads:main + 2 * heads].reshape(d, groups, hb)
    w_gate = jnp.concatenate([wb, wa], axis=2).reshape(d, 2 * heads)
    w_gate = jnp.pad(w_gate, ((0, 0), (0, GATE_W - 2 * heads))).astype(BF16)

    def per_col(vals, fill):
        zeros = jnp.full((groups, hb), fill, F32)
        cols = jnp.concatenate([zeros, vals.astype(F32).reshape(groups, hb)], axis=1).reshape(1, 2 * heads)
        return jnp.pad(cols, ((0, 0), (0, GATE_W - 2 * heads)), constant_values=fill)

    is_a = per_col(jnp.ones((heads,), F32), 0.0)
    neg_a = per_col(-jnp.exp(a_log.astype(F32)), 0.0)
    dtb = per_col(dt_bias, 0.0)
    p, gates = _in_proj(x2, w_in_b16, layer, main, w_gate, is_a, neg_a, dtb, tiles["tm"], tiles["tn"])
    gates = gates[:, :2 * heads]
    gcol = gates.reshape(bsz, seq, groups, 2 * hb).transpose(0, 2, 1, 3)
    grow = gates.reshape(bsz, seq // CHUNK, CHUNK, groups, 2 * hb).transpose(0, 3, 1, 4, 2)
    o = _gdn_core(p.reshape(bsz, seq, main), conv_w, gcol, grow, norm_w.reshape(1, HEAD_W), hb,
                  tiles["gdn_chunks"])
    return _proj_ln(o.reshape(t, v_w), w_out_b16, layer, x2, ln_g.reshape(1, d), ln_b.reshape(1, d),
                    alpha, tiles["tm_proj"])


def _tiles():
    return dict(tm=1024, tn=2048, tm_proj=512, tm_mlp=512, tf_mlp=1024, tq=1024, tk=512, nh=2, hb=16,
                gdn_chunks=8)


def _forward(x, gdn_w_in, gdn_conv_w, gdn_a_log, gdn_dt_bias, gdn_norm_w, gdn_w_out, diff_w_q, diff_lambda,
             diff_subln_w, diff_w_o, shared_w_kv, mlp_w_up, mlp_w_down, ln_g, ln_b, tiles):
    bsz, seq, d = x.shape
    t = bsz * seq
    depth = mlp_w_up.shape[0]
    n_a = gdn_w_in.shape[0]
    alpha = (2 * depth) ** 0.25
    x2 = x.reshape(t, d).astype(F32)
    w_in_b16 = gdn_w_in.astype(BF16)
    w_out_b16 = gdn_w_out.astype(BF16)
    w_q_b16 = diff_w_q.astype(BF16)
    w_o_b16 = diff_w_o.astype(BF16)
    w_up_b16 = mlp_w_up.astype(BF16)
    w_down_b16 = mlp_w_down.astype(BF16)
    dk = diff_lambda.shape[-1]
    q_scale = dk ** -0.5 * math.log2(math.e)
    kv_w = shared_w_kv.shape[1]
    attn_heads = diff_w_q.shape[2] // HEAD_W
    w_kvq_b16 = jnp.concatenate([shared_w_kv.astype(BF16), w_q_b16[0]], axis=1)
    kvq_scale = jnp.concatenate([jnp.ones((1, kv_w), F32), jnp.full((1, diff_w_q.shape[2]), q_scale, F32)], axis=1)
    kv3 = None
    for l in range(depth):
        if l < n_a:
            x2 = _gdn_layer(x2, bsz, seq, l, gdn_w_in, w_in_b16, gdn_conv_w[l], gdn_a_log[l], gdn_dt_bias[l],
                            gdn_norm_w[l], w_out_b16, ln_g[l, 0], ln_b[l, 0], alpha,
                            min(tiles["hb"], gdn_a_log.shape[1]), tiles)
        else:
            j = l - n_a
            lambda_init = 0.8 - 0.6 * math.exp(-0.3 * l)
            if j == 0:
                q3, q_col0 = kv3, kv_w
            else:
                q = _matmul(x2, w_q_b16, j, BF16, tiles["tm"], tiles["tn"], scale=q_scale, name="diff_q_proj")
                q3, q_col0 = q.reshape(bsz, seq, -1), 0
            o = _diff_attention(q3, q_col0, kv3, attn_heads, diff_lambda[j].astype(F32),
                                diff_subln_w[j].reshape(1, HEAD_W).astype(F32), lambda_init, tiles["tq"],
                                tiles["tk"], tiles["nh"])
            x2 = _proj_ln(o.reshape(t, -1), w_o_b16, j, x2, ln_g[l, 0].reshape(1, d),
                          ln_b[l, 0].reshape(1, d), alpha, tiles["tm_proj"])
        x2 = _mlp_ln(x2, w_up_b16, w_down_b16, l, ln_g[l, 1].reshape(1, d),
                     ln_b[l, 1].reshape(1, d), alpha, tiles["tm_mlp"], tiles["tf_mlp"])
        if l == n_a - 1:
            kvq = _matmul_colscale(x2, w_kvq_b16, kvq_scale, BF16, tiles["tm"], tiles["tn"], "shared_kvq_proj")
            kv3 = kvq.reshape(bsz, seq, -1)
    return x2.reshape(bsz, seq, d).astype(x.dtype)


def kernel(x, gdn_w_in, gdn_conv_w, gdn_a_log, gdn_dt_bias, gdn_norm_w, gdn_w_out, diff_w_q, diff_lambda,
           diff_subln_w, diff_w_o, shared_w_kv, mlp_w_up, mlp_w_down, ln_g, ln_b):
    return _forward(x, gdn_w_in, gdn_conv_w, gdn_a_log, gdn_dt_bias, gdn_norm_w, gdn_w_out, diff_w_q,
                    diff_lambda, diff_subln_w, diff_w_o, shared_w_kv, mlp_w_up, mlp_w_down, ln_g, ln_b,
                    _tiles())
```
